```python
import math
import jax, jax.numpy as jnp
from jax import lax
import numpy as np

D_MODEL = 2048
BATCH = 4
SEQ = 8192
DEPTH = 2

N_MIXERS = 2
N_CONV_LAYERS = (DEPTH + 1) // 2
N_SB_LAYERS = DEPTH // 2
N_HEADS = 16
HEAD_DIM = D_MODEL // N_HEADS
CONV_WIDTH = 31
FFN_CONV_WIDTH = 3
D_FF = 5632
BLOCK_Q = 128
RMS_EPS = 1e-6
LN_EPS = 1e-5

kernel_name = "hybrid_conformer_stickbreaking_convffn_adaln"


def rms_norm(x, g):
    xf = x.astype(jnp.float32)
    y = xf * lax.rsqrt(jnp.mean(xf * xf, axis=-1, keepdims=True) + RMS_EPS)
    return (y * g.astype(jnp.float32)).astype(x.dtype)


def layer_norm(x, g, b):
    xf = x.astype(jnp.float32)
    mu = jnp.mean(xf, axis=-1, keepdims=True)
    var = jnp.mean(jnp.square(xf - mu), axis=-1, keepdims=True)
    y = (xf - mu) * lax.rsqrt(var + LN_EPS)
    return (y * g.astype(jnp.float32) + b.astype(jnp.float32)).astype(x.dtype)


def ada_ln(c, w, b):
    mod = jax.nn.silu(c) @ w + b
    shift, scale, gate = jnp.split(mod, 3, axis=-1)
    return shift[:, None, :], scale[:, None, :], gate[:, None, :]


def causal_depthwise_conv(x, w, b):
    width, ch = w.shape
    y = lax.conv_general_dilated(
        x, w[:, None, :].astype(x.dtype), window_strides=(1,),
        padding=[(width - 1, 0)], dimension_numbers=('NWC', 'WIO', 'NWC'),
        feature_group_count=ch)
    return y + b


def conformer_conv_module(h, pw1_w, pw1_b, dw_w, dw_b, ln_g, ln_b, pw2_w, pw2_b):
    u = h @ pw1_w + pw1_b
    val, gt = jnp.split(u, 2, axis=-1)
    u = val * jax.nn.sigmoid(gt)
    u = causal_depthwise_conv(u, dw_w, dw_b)
    u = jax.nn.silu(layer_norm(u, ln_g, ln_b))
    return u @ pw2_w + pw2_b


def stick_breaking_attention(q, k, v):
    bsz, nh, t_len, dh = q.shape
    n_blk = t_len // BLOCK_Q
    scale = 1.0 / math.sqrt(dh)
    qb = q.reshape(bsz, nh, n_blk, BLOCK_Q, dh).transpose(2, 0, 1, 3, 4)
    kpos = jnp.arange(t_len)

    def one_block(args):
        qi, blk = args
        z = jnp.einsum('bhqd,bhkd->bhqk', qi, k).astype(jnp.float32) * scale
        qpos = blk * BLOCK_Q + jnp.arange(BLOCK_Q)
        mask = kpos[None, :] < qpos[:, None]
        log_beta = jax.nn.log_sigmoid(z)
        log_keep = jnp.where(mask, jax.nn.log_sigmoid(-z), 0.0)
        after = lax.cumsum(log_keep, axis=3, reverse=True) - log_keep
        a = jnp.where(mask, jnp.exp(log_beta + after), 0.0)
        return jnp.einsum('bhqk,bhkd->bhqd', a.astype(v.dtype), v)

    out = lax.map(one_block, (qb, jnp.arange(n_blk)))
    return out.transpose(1, 2, 0, 3, 4).reshape(bsz, nh, t_len, dh)


def stick_breaking_mixer(h, qkv_w, o_w):
    bsz, t_len, d = h.shape
    qkv = (h @ qkv_w).reshape(bsz, t_len, 3, N_HEADS, HEAD_DIM)
    qkv = qkv.transpose(2, 0, 3, 1, 4)
    o = stick_breaking_attention(qkv[0], qkv[1], qkv[2])
    o = o.transpose(0, 2, 1, 3).reshape(bsz, t_len, d)
    return o @ o_w


def conv_ffn(h, up_w, dw_w, dw_b, down_w):
    u = causal_depthwise_conv(h @ up_w, dw_w, dw_b)
    gt, val = jnp.split(u, 2, axis=-1)
    return (jax.nn.silu(gt) * val) @ down_w


def setup_inputs(seed: int = 0) -> dict:
    key = jax.random.key(seed)
    ks = jax.random.split(key, 24)
    D, F = D_MODEL, D_FF
    nrm = jax.random.normal
    f32 = jnp.float32
    return {
        'x': nrm(ks[0], (BATCH, SEQ, D), f32),
        'c': nrm(ks[1], (BATCH, D), f32),
        'mix_norm_g': 1.0 + 0.01 * nrm(ks[2], (DEPTH, D), f32),
        'mix_mod_w': nrm(ks[3], (DEPTH, D, 3 * D), f32) * (0.5 * D ** -0.5),
        'mix_mod_b': 0.01 * nrm(ks[4], (DEPTH, 3 * D), f32),
        'cv_pw1_w': nrm(ks[5], (N_CONV_LAYERS, D, 2 * D), f32) * D ** -0.5,
        'cv_pw1_b': 0.01 * nrm(ks[6], (N_CONV_LAYERS, 2 * D), f32),
        'cv_dw_w': nrm(ks[7], (N_CONV_LAYERS, CONV_WIDTH, D), f32) * CONV_WIDTH ** -0.5,
        'cv_dw_b': 0.01 * nrm(ks[8], (N_CONV_LAYERS, D), f32),
        'cv_ln_g': 1.0 + 0.01 * nrm(ks[9], (N_CONV_LAYERS, D), f32),
        'cv_ln_b': 0.01 * nrm(ks[10], (N_CONV_LAYERS, D), f32),
        'cv_pw2_w': nrm(ks[11], (N_CONV_LAYERS, D, D), f32) * D ** -0.5,
        'cv_pw2_b': 0.01 * nrm(ks[12], (N_CONV_LAYERS, D), f32),
        'sb_qkv_w': nrm(ks[13], (N_SB_LAYERS, D, 3 * D), f32) * D ** -0.5,
        'sb_o_w': nrm(ks[14], (N_SB_LAYERS, D, D), f32) * D ** -0.5,
        'ffn_norm_g': 1.0 + 0.01 * nrm(ks[15], (DEPTH, D), f32),
        'ffn_mod_w': nrm(ks[16], (DEPTH, D, 3 * D), f32) * (0.5 * D ** -0.5),
        'ffn_mod_b': 0.01 * nrm(ks[17], (DEPTH, 3 * D), f32),
        'ffn_up_w': nrm(ks[18], (DEPTH, D, 2 * F), f32) * D ** -0.5,
        'ffn_dw_w': nrm(ks[19], (DEPTH, FFN_CONV_WIDTH, 2 * F), f32) * FFN_CONV_WIDTH ** -0.5,
        'ffn_dw_b': 0.01 * nrm(ks[20], (DEPTH, 2 * F), f32),
        'ffn_down_w': nrm(ks[21], (DEPTH, F, D), f32) * F ** -0.5,
        'final_norm_g': 1.0 + 0.01 * nrm(ks[22], (D,), f32),
    }


def reference(x, c, mix_norm_g, mix_mod_w, mix_mod_b, cv_pw1_w, cv_pw1_b, cv_dw_w,
              cv_dw_b, cv_ln_g, cv_ln_b, cv_pw2_w, cv_pw2_b, sb_qkv_w, sb_o_w,
              ffn_norm_g, ffn_mod_w, ffn_mod_b, ffn_up_w, ffn_dw_w, ffn_dw_b,
              ffn_down_w, final_norm_g):
    for i in range(DEPTH):
        j = i // N_MIXERS
        shift, scale, gate = ada_ln(c, mix_mod_w[i], mix_mod_b[i])
        h = rms_norm(x, mix_norm_g[i]) * (1.0 + scale) + shift
        if i % N_MIXERS == 0:
            y = conformer_conv_module(h, cv_pw1_w[j], cv_pw1_b[j], cv_dw_w[j], cv_dw_b[j],
                                      cv_ln_g[j], cv_ln_b[j], cv_pw2_w[j], cv_pw2_b[j])
        else:
            y = stick_breaking_mixer(h, sb_qkv_w[j], sb_o_w[j])
        x = x + gate * y
        shift, scale, gate = ada_ln(c, ffn_mod_w[i], ffn_mod_b[i])
        h = rms_norm(x, ffn_norm_g[i]) * (1.0 + scale) + shift
        x = x + gate * conv_ffn(h, ffn_up_w[i], ffn_dw_w[i], ffn_dw_b[i], ffn_down_w[i])
    return rms_norm(x, final_norm_g)
```

```python
import functools
import math

import jax
import jax.numpy as jnp
from jax import lax
from jax.experimental import pallas as pl
from jax.experimental.pallas import tpu as pltpu

N_HEADS = 16
RMS_EPS = 1e-6
LN_EPS = 1e-5

F32 = jnp.float32
BF16 = jnp.bfloat16

_VMEM_LIMIT_BYTES = 56 * 1024 * 1024
_SUBLANES = 8
_CONV_ROW_BLOCK = 32


def _params(n_axes):
    return pltpu.CompilerParams(
        dimension_semantics=("arbitrary",) * n_axes,
        vmem_limit_bytes=_VMEM_LIMIT_BYTES)


def _rms_mod(x, g, shift, scale):
    ms = jnp.mean(x * x, axis=-1, keepdims=True)
    y = x * lax.rsqrt(ms + RMS_EPS)
    return (y * g) * (1.0 + scale) + shift


def _silu(x):
    return x * jax.nn.sigmoid(x)


def _adaln_kernel(c_ref, w_ref, b_ref, o_ref):
    c = c_ref[...]
    s = _silu(c)
    o_ref[...] = jnp.dot(s, w_ref[...], preferred_element_type=F32,
                         precision=lax.Precision.HIGHEST) + b_ref[...]


def _adaln(c_pad, w, b, *, tn=512):
    n_layers, d, d3 = w.shape
    bp = c_pad.shape[0]
    return pl.pallas_call(
        _adaln_kernel,
        grid=(n_layers, d3 // tn),
        in_specs=[
            pl.BlockSpec((bp, d), lambda l, j: (0, 0)),
            pl.BlockSpec((None, d, tn), lambda l, j: (l, 0, j)),
            pl.BlockSpec((None, 1, tn), lambda l, j: (l, 0, j)),
        ],
        out_specs=pl.BlockSpec((None, bp, tn), lambda l, j: (l, 0, j)),
        out_shape=jax.ShapeDtypeStruct((n_layers, bp, d3), F32),
        compiler_params=_params(2),
        name="adaln_mod",
    )(c_pad, w, b.reshape(n_layers, 1, d3))


def _causal_dwconv_rows(ubuf_ref, w_ref, bias, *, pad, width, tm, row_block):
    outs = []
    for r0 in range(0, tm, row_block):
        acc = None
        for k in range(width):
            start = pad + r0 - (width - 1) + k
            term = w_ref[k:k + 1, :] * ubuf_ref[start:start + row_block, :]
            acc = term if acc is None else acc + term
        outs.append(acc + bias)
    return outs


def _stage_with_halo(ubuf_ref, carry_ref, u, *, i, j, pad, tm):
    @pl.when(i == 0)
    def _():
        ubuf_ref[0:pad, :] = jnp.zeros((pad, u.shape[1]), F32)

    @pl.when(i > 0)
    def _():
        ubuf_ref[0:pad, :] = carry_ref[j]

    ubuf_ref[pad:pad + tm, :] = u
    carry_ref[j] = u[tm - pad:tm, :]


def _pw1_kernel(x_ref, g_ref, shift_ref, scale_ref, wv_ref, wg_ref, bv_ref, bg_ref,
                dww_ref, dwb_ref, o_ref, h_ref, carry_ref, ubuf_ref, *, tm, pad, width):
    i = pl.program_id(1)
    j = pl.program_id(2)

    @pl.when(j == 0)
    def _():
        h_ref[...] = _rms_mod(x_ref[...], g_ref[...], shift_ref[...], scale_ref[...]).astype(BF16)

    h = h_ref[...]
    val = jnp.dot(h, wv_ref[...], preferred_element_type=F32) + bv_ref[...]
    gt = jnp.dot(h, wg_ref[...], preferred_element_type=F32) + bg_ref[...]
    u = val * jax.nn.sigmoid(gt)
    _stage_with_halo(ubuf_ref, carry_ref, u, i=i, j=j, pad=pad, tm=tm)
    outs = _causal_dwconv_rows(ubuf_ref, dww_ref, dwb_ref[...], pad=pad, width=width, tm=tm,
                               row_block=_CONV_ROW_BLOCK)
    for n, y in enumerate(outs):
        o_ref[n * _CONV_ROW_BLOCK:(n + 1) * _CONV_ROW_BLOCK, :] = y


def _conformer_in(x, g, mod, w_bf, b, dw_w, dw_b, *, tm=512, tn=512):
    bsz, t_len, d = x.shape
    width = dw_w.shape[0]
    pad = -(-(width - 1) // _SUBLANES) * _SUBLANES
    nj = d // tn
    kern = functools.partial(_pw1_kernel, tm=tm, pad=pad, width=width)
    return pl.pallas_call(
        kern,
        grid=(bsz, t_len // tm, nj),
        in_specs=[
            pl.BlockSpec((None, tm, d), lambda b_, i, j: (b_, i, 0)),
            pl.BlockSpec((1, d), lambda b_, i, j: (0, 0)),
            pl.BlockSpec((None, 1, d), lambda b_, i, j: (b_, 0, 0)),
            pl.BlockSpec((None, 1, d), lambda b_, i, j: (b_, 0, 1)),
            pl.BlockSpec((d, tn), lambda b_, i, j: (0, j)),
            pl.BlockSpec((d, tn), lambda b_, i, j: (0, nj + j)),
            pl.BlockSpec((1, tn), lambda b_, i, j: (0, j)),
            pl.BlockSpec((1, tn), lambda b_, i, j: (0, nj + j)),
            pl.BlockSpec((width, tn), lambda b_, i, j: (0, j)),
            pl.BlockSpec((1, tn), lambda b_, i, j: (0, j)),
        ],
        out_specs=pl.BlockSpec((None, tm, tn), lambda b_, i, j: (b_, i, j)),
        out_shape=jax.ShapeDtypeStruct((bsz, t_len, d), F32),
        scratch_shapes=[
            pltpu.VMEM((tm, d), BF16),
            pltpu.VMEM((nj, pad, tn), F32),
            pltpu.VMEM((pad + tm, tn), F32),
        ],
        compiler_params=_params(3),
        name="conformer_pw1_glu_conv",
    )(x, g.reshape(1, d), mod, mod, w_bf, w_bf, b.reshape(1, -1), b.reshape(1, -1),
      dw_w, dw_b.reshape(1, d))


def _proj_res_kernel(a_ref, w_ref, b_ref, x_ref, gate_ref, o_ref):
    y = jnp.dot(a_ref[...], w_ref[...], preferred_element_type=F32) + b_ref[...]
    o_ref[...] = x_ref[...] + gate_ref[...] * y


def _ln_proj_res_kernel(c_ref, lg_ref, lb_ref, w_ref, b_ref, x_ref, gate_ref, o_ref, h_ref):
    j = pl.program_id(2)

    @pl.when(j == 0)
    def _():
        c = c_ref[...]
        mu = jnp.mean(c, axis=-1, keepdims=True)
        cc = c - mu
        var = jnp.mean(cc * cc, axis=-1, keepdims=True)
        y = cc * lax.rsqrt(var + LN_EPS) * lg_ref[...] + lb_ref[...]
        h_ref[...] = _silu(y).astype(BF16)

    y = jnp.dot(h_ref[...], w_ref[...], preferred_element_type=F32) + b_ref[...]
    o_ref[...] = x_ref[...] + gate_ref[...] * y


def _proj_residual(a, w_bf, b, x, mod, *, ln=None, tm=512, tn=512):
    bsz, t_len, kdim = a.shape
    d = x.shape[-1]
    nj = d // tn
    gate_blk = 2 * nj
    common = [
        pl.BlockSpec((kdim, tn), lambda b_, i, j: (0, j)),
        pl.BlockSpec((1, tn), lambda b_, i, j: (0, j)),
        pl.BlockSpec((None, tm, tn), lambda b_, i, j: (b_, i, j)),
        pl.BlockSpec((None, 1, tn), lambda b_, i, j: (b_, 0, gate_blk + j)),
    ]
    a_spec = pl.BlockSpec((None, tm, kdim), lambda b_, i, j: (b_, i, 0))
    if ln is None:
        kern, in_specs, args, scratch = _proj_res_kernel, [a_spec] + common, (a,), []
        name = "proj_residual"
    else:
        row = pl.BlockSpec((1, kdim), lambda b_, i, j: (0, 0))
        kern, in_specs = _ln_proj_res_kernel, [a_spec, row, row] + common
        args = (a, ln[0].reshape(1, kdim), ln[1].reshape(1, kdim))
        scratch = [pltpu.VMEM((tm, kdim), BF16)]
        name = "ln_silu_proj_residual"
    return pl.pallas_call(
        kern,
        grid=(bsz, t_len // tm, nj),
        in_specs=in_specs,
        out_specs=pl.BlockSpec((None, tm, tn), lambda b_, i, j: (b_, i, j)),
        out_shape=jax.ShapeDtypeStruct(x.shape, F32),
        scratch_shapes=scratch,
        compiler_params=_params(3),
        name=name,
    )(*args, w_bf, b.reshape(1, d), x, mod)


def _ffn_up_kernel(x_ref, g_ref, shift_ref, scale_ref, wg_ref, wv_ref, dwg_ref, dwv_ref,
                   bg_ref, bv_ref, o_ref, h_ref, carry_ref, ubuf_ref, *, tm, pad, width):
    i = pl.program_id(1)
    j = pl.program_id(2)

    @pl.when(j == 0)
    def _():
        h_ref[...] = _rms_mod(x_ref[...], g_ref[...], shift_ref[...], scale_ref[...]).astype(BF16)

    h = h_ref[...]
    halves = []
    for half, (w_ref, dw_ref, b_ref) in enumerate(((wg_ref, dwg_ref, bg_ref), (wv_ref, dwv_ref, bv_ref))):
        u = jnp.dot(h, w_ref[...], preferred_element_type=F32)
        _stage_with_halo(ubuf_ref.at[half], carry_ref.at[half], u, i=i, j=j, pad=pad, tm=tm)
        halves.append(_causal_dwconv_rows(ubuf_ref.at[half], dw_ref, b_ref[...], pad=pad, width=width,
                                          tm=tm, row_block=_CONV_ROW_BLOCK))
    for n, (gt, val) in enumerate(zip(*halves)):
        o_ref[n * _CONV_ROW_BLOCK:(n + 1) * _CONV_ROW_BLOCK, :] = (_silu(gt) * val).astype(o_ref.dtype)


def _ffn_up(x, g, mod, w_bf, dw_w, dw_b, *, tm=512, tn=512):
    bsz, t_len, d = x.shape
    f2 = w_bf.shape[1]
    f = f2 // 2
    width = dw_w.shape[0]
    pad = -(-(width - 1) // _SUBLANES) * _SUBLANES
    nj = f // tn
    kern = functools.partial(_ffn_up_kernel, tm=tm, pad=pad, width=width)
    return pl.pallas_call(
        kern,
        grid=(bsz, t_len // tm, nj),
        in_specs=[
            pl.BlockSpec((None, tm, d), lambda b_, i, j: (b_, i, 0)),
            pl.BlockSpec((1, d), lambda b_, i, j: (0, 0)),
            pl.BlockSpec((None, 1, d), lambda b_, i, j: (b_, 0, 0)),
            pl.BlockSpec((None, 1, d), lambda b_, i, j: (b_, 0, 1)),
            pl.BlockSpec((d, tn), lambda b_, i, j: (0, j)),
            pl.BlockSpec((d, tn), lambda b_, i, j: (0, nj + j)),
            pl.BlockSpec((width, tn), lambda b_, i, j: (0, j)),
            pl.BlockSpec((width, tn), lambda b_, i, j: (0, nj + j)),
            pl.BlockSpec((1, tn), lambda b_, i, j: (0, j)),
            pl.BlockSpec((1, tn), lambda b_, i, j: (0, nj + j)),
        ],
        out_specs=pl.BlockSpec((None, tm, tn), lambda b_, i, j: (b_, i, j)),
        out_shape=jax.ShapeDtypeStruct((bsz, t_len, f), BF16),
        scratch_shapes=[
            pltpu.VMEM((tm, d), BF16),
            pltpu.VMEM((2, nj, pad, tn), F32),
            pltpu.VMEM((2, pad + tm, tn), F32),
        ],
        compiler_params=_params(3),
        name="ffn_up_conv_swiglu",
    )(x, g.reshape(1, d), mod, mod, w_bf, w_bf, dw_w, dw_w, dw_b.reshape(1, f2), dw_b.reshape(1, f2))


def _norm_proj_kernel(x_ref, g_ref, shift_ref, scale_ref, w_ref, o_ref, h_ref):
    j = pl.program_id(2)

    @pl.when(j == 0)
    def _():
        h_ref[...] = _rms_mod(x_ref[...], g_ref[...], shift_ref[...], scale_ref[...]).astype(BF16)

    o_ref[...] = jnp.dot(h_ref[...], w_ref[...], preferred_element_type=F32).astype(o_ref.dtype)


def _norm_proj(x, g, mod, w_bf, *, tm=512, tn=512):
    bsz, t_len, d = x.shape
    n_out = w_bf.shape[1]
    return pl.pallas_call(
        _norm_proj_kernel,
        grid=(bsz, t_len // tm, n_out // tn),
        in_specs=[
            pl.BlockSpec((None, tm, d), lambda b_, i, j: (b_, i, 0)),
            pl.BlockSpec((1, d), lambda b_, i, j: (0, 0)),
            pl.BlockSpec((None, 1, d), lambda b_, i, j: (b_, 0, 0)),
            pl.BlockSpec((None, 1, d), lambda b_, i, j: (b_, 0, 1)),
            pl.BlockSpec((d, tn), lambda b_, i, j: (0, j)),
        ],
        out_specs=pl.BlockSpec((None, tm, tn), lambda b_, i, j: (b_, i, j)),
        out_shape=jax.ShapeDtypeStruct((bsz, t_len, n_out), BF16),
        scratch_shapes=[pltpu.VMEM((tm, d), BF16)],
        compiler_params=_params(3),
        name="norm_qkv_proj",
    )(x, g.reshape(1, d), mod, mod, w_bf)


def _sb_attn_kernel(q_ref, k_ref, v_ref, o_ref, *, tq, scale):
    i = pl.program_id(2)
    q = q_ref[...]
    row = lax.broadcasted_iota(jnp.int32, (tq, tq), 0)
    col = lax.broadcasted_iota(jnp.int32, (tq, tq), 1)
    suffix_op = (row > col).astype(BF16)
    causal = col < row

    def chunk(kb, carry, acc, masked):
        start = pl.multiple_of(kb * tq, tq)
        k = k_ref[pl.ds(start, tq), :]
        v = v_ref[pl.ds(start, tq), :]
        z = lax.dot_general(q, k, (((1,), (1,)), ((), ())), preferred_element_type=F32) * scale
        log_beta = jnp.minimum(z, 0.0) - jnp.log1p(jnp.exp(-jnp.abs(z)))
        log_keep = log_beta - z
        if masked:
            log_keep = jnp.where(causal, log_keep, 0.0)
        hi = log_keep.astype(BF16)
        lo = (log_keep - hi.astype(F32)).astype(BF16)
        suffix = (jnp.dot(hi, suffix_op, preferred_element_type=F32)
                  + jnp.dot(lo, suffix_op, preferred_element_type=F32))
        after = carry + suffix
        a = jnp.exp(log_beta + after)
        if masked:
            a = jnp.where(causal, a, 0.0)
        acc = acc + jnp.dot(a.astype(BF16), v, preferred_element_type=F32)
        carry = after[:, 0:1] + log_keep[:, 0:1]
        return carry, acc

    carry = jnp.zeros((tq, 1), F32)
    acc = jnp.zeros((tq, q.shape[1]), F32)
    carry, acc = chunk(i, carry, acc, True)

    def body(n, state):
        return chunk(i - 1 - n, state[0], state[1], False)

    carry, acc = lax.fori_loop(0, i, body, (carry, acc))
    o_ref[...] = acc.astype(o_ref.dtype)


def _sb_attention(qkv, *, tq=256):
    bsz, t_len, d3 = qkv.shape
    d = d3 // 3
    dh = d // N_HEADS
    kern = functools.partial(_sb_attn_kernel, tq=tq, scale=1.0 / math.sqrt(dh))
    return pl.pallas_call(
        kern,
        grid=(bsz, N_HEADS, t_len // tq),
        in_specs=[
            pl.BlockSpec((None, tq, dh), lambda b_, h, i: (b_, i, h)),
            pl.BlockSpec((None, t_len, dh), lambda b_, h, i: (b_, 0, N_HEADS + h)),
            pl.BlockSpec((None, t_len, dh), lambda b_, h, i: (b_, 0, 2 * N_HEADS + h)),
        ],
        out_specs=pl.BlockSpec((None, tq, dh), lambda b_, h, i: (b_, i, h)),
        out_shape=jax.ShapeDtypeStruct((bsz, t_len, d), BF16),
        compiler_params=_params(3),
        name="stick_breaking_attention",
    )(qkv, qkv, qkv)


def _final_norm_kernel(x_ref, g_ref, o_ref):
    x = x_ref[...]
    ms = jnp.mean(x * x, axis=-1, keepdims=True)
    o_ref[...] = x * lax.rsqrt(ms + RMS_EPS) * g_ref[...]


def _final_norm(x, g, *, tm=512):
    bsz, t_len, d = x.shape
    return pl.pallas_call(
        _final_norm_kernel,
        grid=(bsz, t_len // tm),
        in_specs=[
            pl.BlockSpec((None, tm, d), lambda b_, i: (b_, i, 0)),
            pl.BlockSpec((1, d), lambda b_, i: (0, 0)),
        ],
        out_specs=pl.BlockSpec((None, tm, d), lambda b_, i: (b_, i, 0)),
        out_shape=jax.ShapeDtypeStruct(x.shape, F32),
        compiler_params=_params(2),
        name="final_rms_norm",
    )(x, g.reshape(1, d))


def kernel(x, c, mix_norm_g, mix_mod_w, mix_mod_b, cv_pw1_w, cv_pw1_b, cv_dw_w, cv_dw_b, cv_ln_g, cv_ln_b, cv_pw2_w, cv_pw2_b, sb_qkv_w, sb_o_w, ffn_norm_g, ffn_mod_w, ffn_mod_b, ffn_up_w, ffn_dw_w, ffn_dw_b, ffn_down_w, final_norm_g):
    bsz, t_len, d = x.shape
    depth = mix_norm_g.shape[0]
    c_pad = jnp.pad(c, ((0, (-bsz) % _SUBLANES), (0, 0)))
    mix_mod = _adaln(c_pad, mix_mod_w, mix_mod_b)[:, :bsz].reshape(depth, bsz, 1, 3 * d)
    ffn_mod = _adaln(c_pad, ffn_mod_w, ffn_mod_b)[:, :bsz].reshape(depth, bsz, 1, 3 * d)
    zero_bias = jnp.zeros((d,), F32)

    for i in range(depth):
        j = i // 2
        if i % 2 == 0:
            conv = _conformer_in(x, mix_norm_g[i], mix_mod[i], cv_pw1_w[j].astype(BF16), cv_pw1_b[j],
                                 cv_dw_w[j], cv_dw_b[j])
            x = _proj_residual(conv, cv_pw2_w[j].astype(BF16), cv_pw2_b[j], x, mix_mod[i],
                               ln=(cv_ln_g[j], cv_ln_b[j]))
        else:
            qkv = _norm_proj(x, mix_norm_g[i], mix_mod[i], sb_qkv_w[j].astype(BF16))
            o = _sb_attention(qkv)
            x = _proj_residual(o, sb_o_w[j].astype(BF16), zero_bias, x, mix_mod[i])
        act = _ffn_up(x, ffn_norm_g[i], ffn_mod[i], ffn_up_w[i].astype(BF16), ffn_dw_w[i], ffn_dw_b[i])
        x = _proj_residual(act, ffn_down_w[i].astype(BF16), zero_bias, x, ffn_mod[i])
    return _final_norm(x, final_norm_g)
```

```python
import functools
import math

import jax
import jax.numpy as jnp
from jax import lax
from jax.experimental import pallas as pl
from jax.experimental.pallas import tpu as pltpu

N_HEADS = 16
RMS_EPS = 1e-6
LN_EPS = 1e-5

F32 = jnp.float32
BF16 = jnp.bfloat16

_VMEM_LIMIT_BYTES = 56 * 1024 * 1024
_SUBLANES = 8
_CONV_ROW_BLOCK = 32


def _params(n_axes):
    return pltpu.CompilerParams(
        dimension_semantics=("arbitrary",) * n_axes,
        vmem_limit_bytes=_VMEM_LIMIT_BYTES)


def _rms_mod(x, g, shift, scale):
    ms = jnp.mean(x * x, axis=-1, keepdims=True)
    y = x * lax.rsqrt(ms + RMS_EPS)
    return (y * g) * (1.0 + scale) + shift


def _silu(x):
    return x * jax.nn.sigmoid(x)


def _adaln_kernel(c_ref, w_ref, b_ref, o_ref):
    c = c_ref[...]
    s = _silu(c)
    o_ref[...] = jnp.dot(s, w_ref[...], preferred_element_type=F32,
                         precision=lax.Precision.HIGHEST) + b_ref[...]


def _adaln(c_pad, w, b, *, tn=512):
    n_layers, d, d3 = w.shape
    bp = c_pad.shape[0]
    return pl.pallas_call(
        _adaln_kernel,
        grid=(n_layers, d3 // tn),
        in_specs=[
            pl.BlockSpec((bp, d), lambda l, j: (0, 0)),
            pl.BlockSpec((None, d, tn), lambda l, j: (l, 0, j)),
            pl.BlockSpec((None, 1, tn), lambda l, j: (l, 0, j)),
        ],
        out_specs=pl.BlockSpec((None, bp, tn), lambda l, j: (l, 0, j)),
        out_shape=jax.ShapeDtypeStruct((n_layers, bp, d3), F32),
        compiler_params=_params(2),
        name="adaln_mod",
    )(c_pad, w, b.reshape(n_layers, 1, d3))


def _causal_dwconv_rows(ubuf_ref, w_ref, bias, *, pad, width, tm, row_block):
    outs = []
    for r0 in range(0, tm, row_block):
        acc = None
        for k in range(width):
            start = pad + r0 - (width - 1) + k
            term = w_ref[k:k + 1, :] * ubuf_ref[start:start + row_block, :]
            acc = term if acc is None else acc + term
        outs.append(acc + bias)
    return outs


def _stage_with_halo(ubuf_ref, carry_ref, u, *, i, j, pad, tm):
    @pl.when(i == 0)
    def _():
        ubuf_ref[0:pad, :] = jnp.zeros((pad, u.shape[1]), F32)

    @pl.when(i > 0)
    def _():
        ubuf_ref[0:pad, :] = carry_ref[j]

    ubuf_ref[pad:pad + tm, :] = u
    carry_ref[j] = u[tm - pad:tm, :]


def _pw1_kernel(x_ref, g_ref, shift_ref, scale_ref, wv_ref, wg_ref, bv_ref, bg_ref,
                dww_ref, dwb_ref, o_ref, h_ref, carry_ref, ubuf_ref, *, tm, pad, width):
    i = pl.program_id(1)
    j = pl.program_id(2)

    @pl.when(j == 0)
    def _():
        h_ref[...] = _rms_mod(x_ref[...], g_ref[...], shift_ref[...], scale_ref[...]).astype(BF16)

    h = h_ref[...]
    val = jnp.dot(h, wv_ref[...], preferred_element_type=F32) + bv_ref[...]
    gt = jnp.dot(h, wg_ref[...], preferred_element_type=F32) + bg_ref[...]
    u = val * jax.nn.sigmoid(gt)
    _stage_with_halo(ubuf_ref, carry_ref, u, i=i, j=j, pad=pad, tm=tm)
    outs = _causal_dwconv_rows(ubuf_ref, dww_ref, dwb_ref[...], pad=pad, width=width, tm=tm,
                               row_block=_CONV_ROW_BLOCK)
    for n, y in enumerate(outs):
        o_ref[n * _CONV_ROW_BLOCK:(n + 1) * _CONV_ROW_BLOCK, :] = y


def _conformer_in(x, g, mod, w_bf, b, dw_w, dw_b, *, tm=512, tn=512):
    bsz, t_len, d = x.shape
    width = dw_w.shape[0]
    pad = -(-(width - 1) // _SUBLANES) * _SUBLANES
    nj = d // tn
    kern = functools.partial(_pw1_kernel, tm=tm, pad=pad, width=width)
    return pl.pallas_call(
        kern,
        grid=(bsz, t_len // tm, nj),
        in_specs=[
            pl.BlockSpec((None, tm, d), lambda b_, i, j: (b_, i, 0)),
            pl.BlockSpec((1, d), lambda b_, i, j: (0, 0)),
            pl.BlockSpec((None, 1, d), lambda b_, i, j: (b_, 0, 0)),
            pl.BlockSpec((None, 1, d), lambda b_, i, j: (b_, 0, 1)),
            pl.BlockSpec((d, tn), lambda b_, i, j: (0, j)),
            pl.BlockSpec((d, tn), lambda b_, i, j: (0, nj + j)),
            pl.BlockSpec((1, tn), lambda b_, i, j: (0, j)),
            pl.BlockSpec((1, tn), lambda b_, i, j: (0, nj + j)),
            pl.BlockSpec((width, tn), lambda b_, i, j: (0, j)),
            pl.BlockSpec((1, tn), lambda b_, i, j: (0, j)),
        ],
        out_specs=pl.BlockSpec((None, tm, tn), lambda b_, i, j: (b_, i, j)),
        out_shape=jax.ShapeDtypeStruct((bsz, t_len, d), F32),
        scratch_shapes=[
            pltpu.VMEM((tm, d), BF16),
            pltpu.VMEM((nj, pad, tn), F32),
            pltpu.VMEM((pad + tm, tn), F32),
        ],
        compiler_params=_params(3),
        name="conformer_pw1_glu_conv",
    )(x, g.reshape(1, d), mod, mod, w_bf, w_bf, b.reshape(1, -1), b.reshape(1, -1),
      dw_w, dw_b.reshape(1, d))


def _proj_res_kernel(a_ref, w_ref, b_ref, x_ref, gate_ref, o_ref):
    y = jnp.dot(a_ref[...], w_ref[...], preferred_element_type=F32) + b_ref[...]
    o_ref[...] = x_ref[...] + gate_ref[...] * y


def _ln_proj_res_kernel(c_ref, lg_ref, lb_ref, w_ref, b_ref, x_ref, gate_ref, o_ref, h_ref):
    j = pl.program_id(2)

    @pl.when(j == 0)
    def _():
        c = c_ref[...]
        mu = jnp.mean(c, axis=-1, keepdims=True)
        cc = c - mu
        var = jnp.mean(cc * cc, axis=-1, keepdims=True)
        y = cc * lax.rsqrt(var + LN_EPS) * lg_ref[...] + lb_ref[...]
        h_ref[...] = _silu(y).astype(BF16)

    y = jnp.dot(h_ref[...], w_ref[...], preferred_element_type=F32) + b_ref[...]
    o_ref[...] = x_ref[...] + gate_ref[...] * y


def _proj_residual(a, w_bf, b, x, mod, *, ln=None, tm=512, tn=512):
    bsz, t_len, kdim = a.shape
    d = x.shape[-1]
    nj = d // tn
    gate_blk = 2 * nj
    common = [
        pl.BlockSpec((kdim, tn), lambda b_, i, j: (0, j)),
        pl.BlockSpec((1, tn), lambda b_, i, j: (0, j)),
        pl.BlockSpec((None, tm, tn), lambda b_, i, j: (b_, i, j)),
        pl.BlockSpec((None, 1, tn), lambda b_, i, j: (b_, 0, gate_blk + j)),
    ]
    a_spec = pl.BlockSpec((None, tm, kdim), lambda b_, i, j: (b_, i, 0))
    if ln is None:
        kern, in_specs, args, scratch = _proj_res_kernel, [a_spec] + common, (a,), []
        name = "proj_residual"
    else:
        row = pl.BlockSpec((1, kdim), lambda b_, i, j: (0, 0))
        kern, in_specs = _ln_proj_res_kernel, [a_spec, row, row] + common
        args = (a, ln[0].reshape(1, kdim), ln[1].reshape(1, kdim))
        scratch = [pltpu.VMEM((tm, kdim), BF16)]
        name = "ln_silu_proj_residual"
    return pl.pallas_call(
        kern,
        grid=(bsz, t_len // tm, nj),
        in_specs=in_specs,
        out_specs=pl.BlockSpec((None, tm, tn), lambda b_, i, j: (b_, i, j)),
        out_shape=jax.ShapeDtypeStruct(x.shape, F32),
        scratch_shapes=scratch,
        compiler_params=_params(3),
        name=name,
    )(*args, w_bf, b.reshape(1, d), x, mod)


def _ffn_up_kernel(x_ref, g_ref, shift_ref, scale_ref, wg_ref, wv_ref, dwg_ref, dwv_ref,
                   bg_ref, bv_ref, o_ref, h_ref, carry_ref, ubuf_ref, *, tm, pad, width):
    i = pl.program_id(1)
    j = pl.program_id(2)

    @pl.when(j == 0)
    def _():
        h_ref[...] = _rms_mod(x_ref[...], g_ref[...], shift_ref[...], scale_ref[...]).astype(BF16)

    h = h_ref[...]
    halves = []
    for half, (w_ref, dw_ref, b_ref) in enumerate(((wg_ref, dwg_ref, bg_ref), (wv_ref, dwv_ref, bv_ref))):
        u = jnp.dot(h, w_ref[...], preferred_element_type=F32)
        _stage_with_halo(ubuf_ref.at[half], carry_ref.at[half], u, i=i, j=j, pad=pad, tm=tm)
        halves.append(_causal_dwconv_rows(ubuf_ref.at[half], dw_ref, b_ref[...], pad=pad, width=width,
                                          tm=tm, row_block=_CONV_ROW_BLOCK))
    for n, (gt, val) in enumerate(zip(*halves)):
        o_ref[n * _CONV_ROW_BLOCK:(n + 1) * _CONV_ROW_BLOCK, :] = (_silu(gt) * val).astype(o_ref.dtype)


def _ffn_up(x, g, mod, w_bf, dw_w, dw_b, *, tm=512, tn=512):
    bsz, t_len, d = x.shape
    f2 = w_bf.shape[1]
    f = f2 // 2
    width = dw_w.shape[0]
    pad = -(-(width - 1) // _SUBLANES) * _SUBLANES
    nj = f // tn
    kern = functools.partial(_ffn_up_kernel, tm=tm, pad=pad, width=width)
    return pl.pallas_call(
        kern,
        grid=(bsz, t_len // tm, nj),
        in_specs=[
            pl.BlockSpec((None, tm, d), lambda b_, i, j: (b_, i, 0)),
            pl.BlockSpec((1, d), lambda b_, i, j: (0, 0)),
            pl.BlockSpec((None, 1, d), lambda b_, i, j: (b_, 0, 0)),
            pl.BlockSpec((None, 1, d), lambda b_, i, j: (b_, 0, 1)),
            pl.BlockSpec((d, tn), lambda b_, i, j: (0, j)),
            pl.BlockSpec((d, tn), lambda b_, i, j: (0, nj + j)),
            pl.BlockSpec((width, tn), lambda b_, i, j: (0, j)),
            pl.BlockSpec((width, tn), lambda b_, i, j: (0, nj + j)),
            pl.BlockSpec((1, tn), lambda b_, i, j: (0, j)),
            pl.BlockSpec((1, tn), lambda b_, i, j: (0, nj + j)),
        ],
        out_specs=pl.BlockSpec((None, tm, tn), lambda b_, i, j: (b_, i, j)),
        out_shape=jax.ShapeDtypeStruct((bsz, t_len, f), BF16),
        scratch_shapes=[
            pltpu.VMEM((tm, d), BF16),
            pltpu.VMEM((2, nj, pad, tn), F32),
            pltpu.VMEM((2, pad + tm, tn), F32),
        ],
        compiler_params=_params(3),
        name="ffn_up_conv_swiglu",
    )(x, g.reshape(1, d), mod, mod, w_bf, w_bf, dw_w, dw_w, dw_b.reshape(1, f2), dw_b.reshape(1, f2))


def _norm_proj_kernel(x_ref, g_ref, shift_ref, scale_ref, w_ref, o_ref, h_ref, *, q_tiles, q_scale):
    j = pl.program_id(2)

    @pl.when(j == 0)
    def _():
        h_ref[...] = _rms_mod(x_ref[...], g_ref[...], shift_ref[...], scale_ref[...]).astype(BF16)

    y = jnp.dot(h_ref[...], w_ref[...], preferred_element_type=F32)
    o_ref[...] = (y * jnp.where(j < q_tiles, q_scale, 1.0)).astype(o_ref.dtype)


def _norm_proj(x, g, mod, w_bf, *, tm=512, tn=512):
    bsz, t_len, d = x.shape
    n_out = w_bf.shape[1]
    kern = functools.partial(_norm_proj_kernel, q_tiles=d // tn,
                             q_scale=1.0 / math.sqrt(d // N_HEADS))
    return pl.pallas_call(
        kern,
        grid=(bsz, t_len // tm, n_out // tn),
        in_specs=[
            pl.BlockSpec((None, tm, d), lambda b_, i, j: (b_, i, 0)),
            pl.BlockSpec((1, d), lambda b_, i, j: (0, 0)),
            pl.BlockSpec((None, 1, d), lambda b_, i, j: (b_, 0, 0)),
            pl.BlockSpec((None, 1, d), lambda b_, i, j: (b_, 0, 1)),
            pl.BlockSpec((d, tn), lambda b_, i, j: (0, j)),
        ],
        out_specs=pl.BlockSpec((None, tm, tn), lambda b_, i, j: (b_, i, j)),
        out_shape=jax.ShapeDtypeStruct((bsz, t_len, n_out), BF16),
        scratch_shapes=[pltpu.VMEM((tm, d), BF16)],
        compiler_params=_params(3),
        name="norm_qkv_proj",
    )(x, g.reshape(1, d), mod, mod, w_bf)


_HEADS_PER_STEP = 4
_KEY_CHUNK = 256
_SUFFIX_BLOCK = 128
_SIGN_BIT = 0x80000000


def _neg_abs(x):
    bits = lax.bitcast_convert_type(x, jnp.uint32) | jnp.uint32(_SIGN_BIT)
    return lax.bitcast_convert_type(bits, F32)


def _sb_attn_kernel(q_ref, k_ref, v_ref, o_ref, acc_ref, carry_ref, *, tq, dh, n_grp):
    i = pl.program_id(2)
    tk, sb = _KEY_CHUNK, _SUFFIX_BLOCK
    n_diag = tq // tk
    row = lax.broadcasted_iota(jnp.int32, (2 * sb, 2 * sb), 0)
    col = lax.broadcasted_iota(jnp.int32, (2 * sb, 2 * sb), 1)
    suffix_op = (((row % sb) >= col) | (col >= sb)).astype(BF16)
    qpos = i * tq + lax.broadcasted_iota(jnp.int32, (tq, tk), 0)
    kcol = lax.broadcasted_iota(jnp.int32, (tq, tk), 1)

    carry_ref[...] = jnp.zeros(carry_ref.shape, F32)
    acc_ref[...] = jnp.zeros(acc_ref.shape, F32)

    def chunk(kb, masked):
        start = pl.multiple_of(kb * tk, tk)
        if masked:
            causal = (start + kcol) < qpos
        for g in range(n_grp):
            lanes = slice(g * dh, (g + 1) * dh)
            q = q_ref[:, lanes]
            k = k_ref[pl.ds(start, tk), lanes]
            v = v_ref[pl.ds(start, tk), lanes]
            z = lax.dot_general(q, k, (((1,), (1,)), ((), ())), preferred_element_type=F32)
            p = jnp.maximum(z, 0.0) + jnp.log(1.0 + jnp.exp(_neg_abs(z)))
            if masked:
                p = jnp.where(causal, p, 0.0)
            hi = p.astype(BF16)
            lo = (p - hi.astype(F32)).astype(BF16)
            tail = carry_ref[g]
            logits = [None] * (tk // sb)
            for blk in reversed(range(tk // sb)):
                cols = slice(blk * sb, (blk + 1) * sb)
                stacked = jnp.concatenate([hi[:, cols], lo[:, cols]], axis=1)
                sums = jnp.dot(stacked, suffix_op, preferred_element_type=F32)
                logits[blk] = (z[:, cols] - tail) - sums[:, :sb]
                tail = tail + sums[:, sb:]
            a = jnp.exp(jnp.concatenate(logits, axis=1))
            if masked:
                a = jnp.where(causal, a, 0.0)
            acc_ref[g] += jnp.dot(a.astype(BF16), v, preferred_element_type=F32)
            carry_ref[g] = tail

    for dg in range(n_diag):
        chunk(i * n_diag + (n_diag - 1 - dg), True)

    def body(n, _):
        chunk(i * n_diag - 1 - n, False)
        return 0

    lax.fori_loop(0, i * n_diag, body, 0)
    for g in range(n_grp):
        o_ref[:, g * dh:(g + 1) * dh] = acc_ref[g].astype(o_ref.dtype)


def _sb_attention(qkv, *, tq=512):
    bsz, t_len, d3 = qkv.shape
    d = d3 // 3
    dh = d // N_HEADS
    n_grp = _HEADS_PER_STEP
    wblk = n_grp * dh
    nblk = d // wblk
    kern = functools.partial(_sb_attn_kernel, tq=tq, dh=dh, n_grp=n_grp)
    return pl.pallas_call(
        kern,
        grid=(bsz, nblk, t_len // tq),
        in_specs=[
            pl.BlockSpec((None, tq, wblk), lambda b_, h, i: (b_, i, h)),
            pl.BlockSpec((None, t_len, wblk), lambda b_, h, i: (b_, 0, nblk + h)),
            pl.BlockSpec((None, t_len, wblk), lambda b_, h, i: (b_, 0, 2 * nblk + h)),
        ],
        out_specs=pl.BlockSpec((None, tq, wblk), lambda b_, h, i: (b_, i, h)),
        out_shape=jax.ShapeDtypeStruct((bsz, t_len, d), BF16),
        scratch_shapes=[
            pltpu.VMEM((n_grp, tq, dh), F32),
            pltpu.VMEM((n_grp, tq, _SUFFIX_BLOCK), F32),
        ],
        compiler_params=_params(3),
        name="stick_breaking_attention",
    )(qkv, qkv, qkv)


def _final_norm_kernel(x_ref, g_ref, o_ref):
    x = x_ref[...]
    ms = jnp.mean(x * x, axis=-1, keepdims=True)
    o_ref[...] = x * lax.rsqrt(ms + RMS_EPS) * g_ref[...]


def _final_norm(x, g, *, tm=512):
    bsz, t_len, d = x.shape
    return pl.pallas_call(
        _final_norm_kernel,
        grid=(bsz, t_len // tm),
        in_specs=[
            pl.BlockSpec((None, tm, d), lambda b_, i: (b_, i, 0)),
            pl.BlockSpec((1, d), lambda b_, i: (0, 0)),
        ],
        out_specs=pl.BlockSpec((None, tm, d), lambda b_, i: (b_, i, 0)),
        out_shape=jax.ShapeDtypeStruct(x.shape, F32),
        compiler_params=_params(2),
        name="final_rms_norm",
    )(x, g.reshape(1, d))


def kernel(x, c, mix_norm_g, mix_mod_w, mix_mod_b, cv_pw1_w, cv_pw1_b, cv_dw_w, cv_dw_b, cv_ln_g, cv_ln_b, cv_pw2_w, cv_pw2_b, sb_qkv_w, sb_o_w, ffn_norm_g, ffn_mod_w, ffn_mod_b, ffn_up_w, ffn_dw_w, ffn_dw_b, ffn_down_w, final_norm_g):
    bsz, t_len, d = x.shape
    depth = mix_norm_g.shape[0]
    c_pad = jnp.pad(c, ((0, (-bsz) % _SUBLANES), (0, 0)))
    mix_mod = _adaln(c_pad, mix_mod_w, mix_mod_b)[:, :bsz].reshape(depth, bsz, 1, 3 * d)
    ffn_mod = _adaln(c_pad, ffn_mod_w, ffn_mod_b)[:, :bsz].reshape(depth, bsz, 1, 3 * d)
    zero_bias = jnp.zeros((d,), F32)

    for i in range(depth):
        j = i // 2
        if i % 2 == 0:
            conv = _conformer_in(x, mix_norm_g[i], mix_mod[i], cv_pw1_w[j].astype(BF16), cv_pw1_b[j],
                                 cv_dw_w[j], cv_dw_b[j])
            x = _proj_residual(conv, cv_pw2_w[j].astype(BF16), cv_pw2_b[j], x, mix_mod[i],
                               ln=(cv_ln_g[j], cv_ln_b[j]))
        else:
            qkv = _norm_proj(x, mix_norm_g[i], mix_mod[i], sb_qkv_w[j].astype(BF16))
            o = _sb_attention(qkv)
            x = _proj_residual(o, sb_o_w[j].astype(BF16), zero_bias, x, mix_mod[i])
        act = _ffn_up(x, ffn_norm_g[i], ffn_mod[i], ffn_up_w[i].astype(BF16), ffn_dw_w[i], ffn_dw_b[i])
        x = _proj_residual(act, ffn_down_w[i].astype(BF16), zero_bias, x, ffn_mod[i])
    return _final_norm(x, final_norm_g)
```

```python
import functools
import math

import jax
import jax.numpy as jnp
from jax import lax
from jax.experimental import pallas as pl
from jax.experimental.pallas import tpu as pltpu

N_HEADS = 16
RMS_EPS = 1e-6
LN_EPS = 1e-5

F32 = jnp.float32
BF16 = jnp.bfloat16

_VMEM_LIMIT_BYTES = 56 * 1024 * 1024
_SUBLANES = 8
_LANES = 128
_CONV_ROW_BLOCK = 64


def _params(n_axes):
    return pltpu.CompilerParams(
        dimension_semantics=("arbitrary",) * n_axes,
        vmem_limit_bytes=_VMEM_LIMIT_BYTES)


def _rms_mod(x, g, shift, scale):
    ms = jnp.mean(x * x, axis=-1, keepdims=True)
    return (x * lax.rsqrt(ms + RMS_EPS)) * (g * (1.0 + scale)) + shift


def _silu(x):
    return x * jax.nn.sigmoid(x)


def _adaln_kernel(c_ref, w_ref, b_ref, o_ref):
    c = c_ref[...]
    s = _silu(c)
    o_ref[...] = jnp.dot(s, w_ref[...], preferred_element_type=F32,
                         precision=lax.Precision.HIGHEST) + b_ref[...]


def _adaln(c_pad, w, b, *, tn=512):
    n_layers, d, d3 = w.shape
    bp = c_pad.shape[0]
    return pl.pallas_call(
        _adaln_kernel,
        grid=(n_layers, d3 // tn),
        in_specs=[
            pl.BlockSpec((bp, d), lambda l, j: (0, 0)),
            pl.BlockSpec((None, d, tn), lambda l, j: (l, 0, j)),
            pl.BlockSpec((None, 1, tn), lambda l, j: (l, 0, j)),
        ],
        out_specs=pl.BlockSpec((None, bp, tn), lambda l, j: (l, 0, j)),
        out_shape=jax.ShapeDtypeStruct((n_layers, bp, d3), F32),
        compiler_params=_params(2),
        name="adaln_mod",
    )(c_pad, w, b.reshape(n_layers, 1, d3))


_ROW_STRIDE = 2


def _stage_slab(ubuf_ref, slab, halo, u, *, pad, tm):
    slab_ref = ubuf_ref.at[slab]
    slab_ref[pl.ds(0, pad, stride=_ROW_STRIDE), :] = halo
    slab_ref[pl.ds(_ROW_STRIDE * pad, tm, stride=_ROW_STRIDE), :] = u


def _dwconv_block(ubuf_ref, slab, r0, w_ref, lanes, *, pad, width, rows):
    acc = None
    slab_ref = ubuf_ref.at[slab]
    for k in range(width):
        start = _ROW_STRIDE * (pad + r0 - (width - 1) + k)
        term = w_ref[k:k + 1, lanes] * slab_ref[pl.ds(start, rows, stride=_ROW_STRIDE), :]
        acc = term if acc is None else acc + term
    return acc


def _pw1_kernel(x_ref, g_ref, shift_ref, scale_ref, wv_ref, wg_ref, bv_ref, bg_ref,
                dww_ref, dwb_ref, o_ref, h_ref, carry_ref, ubuf_ref, *, tm, tn_sub, pad, width):
    i = pl.program_id(1)
    j = pl.program_id(2)

    @pl.when(j == 0)
    def _():
        h_ref[...] = _rms_mod(x_ref[...], g_ref[...], shift_ref[...], scale_ref[...]).astype(BF16)

        @pl.when(i == 0)
        def _():
            carry_ref[...] = jnp.zeros(carry_ref.shape, F32)

    h = h_ref[...]
    rb = _CONV_ROW_BLOCK
    for c0 in range(0, o_ref.shape[1], tn_sub):
        cols = slice(c0, c0 + tn_sub)
        val = jnp.dot(h, wv_ref[:, cols], preferred_element_type=F32) + bv_ref[:, cols]
        gt = jnp.dot(h, wg_ref[:, cols], preferred_element_type=F32) + bg_ref[:, cols]
        u = val * jax.nn.sigmoid(gt)
        halo = carry_ref[j, :, cols]
        carry_ref[j, :, cols] = u[tm - pad:tm, :]
        for s in range(tn_sub // _LANES):
            slab = c0 // _LANES + s
            lanes = slice(c0 + s * _LANES, c0 + (s + 1) * _LANES)
            _stage_slab(ubuf_ref, slab, halo[:, s * _LANES:(s + 1) * _LANES],
                        u[:, s * _LANES:(s + 1) * _LANES], pad=pad, tm=tm)
            for r0 in range(0, tm, rb):
                y = _dwconv_block(ubuf_ref, slab, r0, dww_ref, lanes, pad=pad, width=width, rows=rb)
                o_ref[r0:r0 + rb, lanes] = y + dwb_ref[:, lanes]


def _conformer_in(x, g, mod, w_bf, b, dw_w, dw_b, *, tm=512, tn=1024, tn_sub=256):
    bsz, t_len, d = x.shape
    width = dw_w.shape[0]
    pad = -(-(width - 1) // _SUBLANES) * _SUBLANES
    nj = d // tn
    kern = functools.partial(_pw1_kernel, tm=tm, tn_sub=tn_sub, pad=pad, width=width)
    return pl.pallas_call(
        kern,
        grid=(bsz, t_len // tm, nj),
        in_specs=[
            pl.BlockSpec((None, tm, d), lambda b_, i, j: (b_, i, 0)),
            pl.BlockSpec((1, d), lambda b_, i, j: (0, 0)),
            pl.BlockSpec((None, 1, d), lambda b_, i, j: (b_, 0, 0)),
            pl.BlockSpec((None, 1, d), lambda b_, i, j: (b_, 0, 1)),
            pl.BlockSpec((d, tn), lambda b_, i, j: (0, j)),
            pl.BlockSpec((d, tn), lambda b_, i, j: (0, nj + j)),
            pl.BlockSpec((1, tn), lambda b_, i, j: (0, j)),
            pl.BlockSpec((1, tn), lambda b_, i, j: (0, nj + j)),
            pl.BlockSpec((width, tn), lambda b_, i, j: (0, j)),
            pl.BlockSpec((1, tn), lambda b_, i, j: (0, j)),
        ],
        out_specs=pl.BlockSpec((None, tm, tn), lambda b_, i, j: (b_, i, j)),
        out_shape=jax.ShapeDtypeStruct((bsz, t_len, d), F32),
        scratch_shapes=[
            pltpu.VMEM((tm, d), BF16),
            pltpu.VMEM((nj, pad, tn), F32),
            pltpu.VMEM((tn // _LANES, _ROW_STRIDE * (pad + tm), _LANES), F32),
        ],
        compiler_params=_params(3),
        name="conformer_pw1_glu_conv",
    )(x, g.reshape(1, d), mod, mod, w_bf, w_bf, b.reshape(1, -1), b.reshape(1, -1),
      dw_w, dw_b.reshape(1, d))


def _proj_res_kernel(a_ref, w_ref, b_ref, x_ref, gate_ref, o_ref):
    y = jnp.dot(a_ref[...], w_ref[...], preferred_element_type=F32) + b_ref[...]
    o_ref[...] = x_ref[...] + gate_ref[...] * y


def _ln_proj_res_kernel(c_ref, lg_ref, lb_ref, w_ref, b_ref, x_ref, gate_ref, o_ref, h_ref):
    j = pl.program_id(2)

    @pl.when(j == 0)
    def _():
        c = c_ref[...]
        mu = jnp.mean(c, axis=-1, keepdims=True)
        cc = c - mu
        var = jnp.mean(cc * cc, axis=-1, keepdims=True)
        y = cc * lax.rsqrt(var + LN_EPS) * lg_ref[...] + lb_ref[...]
        h_ref[...] = _silu(y).astype(BF16)

    y = jnp.dot(h_ref[...], w_ref[...], preferred_element_type=F32) + b_ref[...]
    o_ref[...] = x_ref[...] + gate_ref[...] * y


def _proj_residual(a, w_bf, b, x, mod, *, ln=None, tm=1024, tn=512):
    bsz, t_len, kdim = a.shape
    d = x.shape[-1]
    nj = d // tn
    gate_blk = 2 * nj
    common = [
        pl.BlockSpec((kdim, tn), lambda b_, i, j: (0, j)),
        pl.BlockSpec((1, tn), lambda b_, i, j: (0, j)),
        pl.BlockSpec((None, tm, tn), lambda b_, i, j: (b_, i, j)),
        pl.BlockSpec((None, 1, tn), lambda b_, i, j: (b_, 0, gate_blk + j)),
    ]
    a_spec = pl.BlockSpec((None, tm, kdim), lambda b_, i, j: (b_, i, 0))
    if ln is None:
        kern, in_specs, args, scratch = _proj_res_kernel, [a_spec] + common, (a,), []
        name = "proj_residual"
    else:
        row = pl.BlockSpec((1, kdim), lambda b_, i, j: (0, 0))
        kern, in_specs = _ln_proj_res_kernel, [a_spec, row, row] + common
        args = (a, ln[0].reshape(1, kdim), ln[1].reshape(1, kdim))
        scratch = [pltpu.VMEM((tm, kdim), BF16)]
        name = "ln_silu_proj_residual"
    return pl.pallas_call(
        kern,
        grid=(bsz, t_len // tm, nj),
        in_specs=in_specs,
        out_specs=pl.BlockSpec((None, tm, tn), lambda b_, i, j: (b_, i, j)),
        out_shape=jax.ShapeDtypeStruct(x.shape, F32),
        scratch_shapes=scratch,
        compiler_params=_params(3),
        name=name,
    )(*args, w_bf, b.reshape(1, d), x, mod)


def _ffn_up_kernel(x_ref, g_ref, shift_ref, scale_ref, wg_ref, wv_ref, dwg_ref, dwv_ref,
                   bg_ref, bv_ref, o_ref, h_ref, carry_ref, ubuf_ref, *, tm, tn_sub, pad, width):
    i = pl.program_id(1)
    j = pl.program_id(2)

    @pl.when(j == 0)
    def _():
        h_ref[...] = _rms_mod(x_ref[...], g_ref[...], shift_ref[...], scale_ref[...]).astype(BF16)

        @pl.when(i == 0)
        def _():
            carry_ref[...] = jnp.zeros(carry_ref.shape, F32)

    h = h_ref[...]
    rb = _CONV_ROW_BLOCK
    sources = ((wg_ref, dwg_ref, bg_ref), (wv_ref, dwv_ref, bv_ref))
    for c0 in range(0, o_ref.shape[1], tn_sub):
        cols = slice(c0, c0 + tn_sub)
        for half, (w_ref, _, _) in enumerate(sources):
            u = jnp.dot(h, w_ref[:, cols], preferred_element_type=F32)
            halo = carry_ref[half, j, :, cols]
            carry_ref[half, j, :, cols] = u[tm - pad:tm, :]
            for s in range(tn_sub // _LANES):
                _stage_slab(ubuf_ref.at[half], c0 // _LANES + s, halo[:, s * _LANES:(s + 1) * _LANES],
                            u[:, s * _LANES:(s + 1) * _LANES], pad=pad, tm=tm)
        for s in range(tn_sub // _LANES):
            slab = c0 // _LANES + s
            lanes = slice(c0 + s * _LANES, c0 + (s + 1) * _LANES)
            for r0 in range(0, tm, rb):
                gt, val = [
                    _dwconv_block(ubuf_ref.at[half], slab, r0, dw_ref, lanes, pad=pad, width=width, rows=rb)
                    + b_ref[:, lanes]
                    for half, (_, dw_ref, b_ref) in enumerate(sources)]
                o_ref[r0:r0 + rb, lanes] = (_silu(gt) * val).astype(o_ref.dtype)


def _ffn_up(x, g, mod, w_bf, dw_w, dw_b, *, tm=1024, tn=512, tn_sub=256):
    bsz, t_len, d = x.shape
    f2 = w_bf.shape[1]
    f = f2 // 2
    width = dw_w.shape[0]
    pad = -(-(width - 1) // _SUBLANES) * _SUBLANES
    nj = f // tn
    kern = functools.partial(_ffn_up_kernel, tm=tm, tn_sub=tn_sub, pad=pad, width=width)
    return pl.pallas_call(
        kern,
        grid=(bsz, t_len // tm, nj),
        in_specs=[
            pl.BlockSpec((None, tm, d), lambda b_, i, j: (b_, i, 0)),
            pl.BlockSpec((1, d), lambda b_, i, j: (0, 0)),
            pl.BlockSpec((None, 1, d), lambda b_, i, j: (b_, 0, 0)),
            pl.BlockSpec((None, 1, d), lambda b_, i, j: (b_, 0, 1)),
            pl.BlockSpec((d, tn), lambda b_, i, j: (0, j)),
            pl.BlockSpec((d, tn), lambda b_, i, j: (0, nj + j)),
            pl.BlockSpec((width, tn), lambda b_, i, j: (0, j)),
            pl.BlockSpec((width, tn), lambda b_, i, j: (0, nj + j)),
            pl.BlockSpec((1, tn), lambda b_, i, j: (0, j)),
            pl.BlockSpec((1, tn), lambda b_, i, j: (0, nj + j)),
        ],
        out_specs=pl.BlockSpec((None, tm, tn), lambda b_, i, j: (b_, i, j)),
        out_shape=jax.ShapeDtypeStruct((bsz, t_len, f), BF16),
        scratch_shapes=[
            pltpu.VMEM((tm, d), BF16),
            pltpu.VMEM((2, nj, pad, tn), F32),
            pltpu.VMEM((2, tn // _LANES, _ROW_STRIDE * (pad + tm), _LANES), F32),
        ],
        compiler_params=_params(3),
        name="ffn_up_conv_swiglu",
    )(x, g.reshape(1, d), mod, mod, w_bf, w_bf, dw_w, dw_w, dw_b.reshape(1, f2), dw_b.reshape(1, f2))


def _norm_proj_kernel(x_ref, g_ref, shift_ref, scale_ref, w_ref, o_ref, h_ref, *, q_tiles, q_scale):
    j = pl.program_id(2)

    @pl.when(j == 0)
    def _():
        h_ref[...] = _rms_mod(x_ref[...], g_ref[...], shift_ref[...], scale_ref[...]).astype(BF16)

    y = jnp.dot(h_ref[...], w_ref[...], preferred_element_type=F32)
    o_ref[...] = (y * jnp.where(j < q_tiles, q_scale, 1.0)).astype(o_ref.dtype)


def _norm_proj(x, g, mod, w_bf, *, tm=1024, tn=1024):
    bsz, t_len, d = x.shape
    n_out = w_bf.shape[1]
    kern = functools.partial(_norm_proj_kernel, q_tiles=d // tn,
                             q_scale=1.0 / math.sqrt(d // N_HEADS))
    return pl.pallas_call(
        kern,
        grid=(bsz, t_len // tm, n_out // tn),
        in_specs=[
            pl.BlockSpec((None, tm, d), lambda b_, i, j: (b_, i, 0)),
            pl.BlockSpec((1, d), lambda b_, i, j: (0, 0)),
            pl.BlockSpec((None, 1, d), lambda b_, i, j: (b_, 0, 0)),
            pl.BlockSpec((None, 1, d), lambda b_, i, j: (b_, 0, 1)),
            pl.BlockSpec((d, tn), lambda b_, i, j: (0, j)),
        ],
        out_specs=pl.BlockSpec((None, tm, tn), lambda b_, i, j: (b_, i, j)),
        out_shape=jax.ShapeDtypeStruct((bsz, t_len, n_out), BF16),
        scratch_shapes=[pltpu.VMEM((tm, d), BF16)],
        compiler_params=_params(3),
        name="norm_qkv_proj",
    )(x, g.reshape(1, d), mod, mod, w_bf)


_HEADS_PER_STEP = 4
_KEY_CHUNK = 256
_SUFFIX_BLOCK = 128
_SIGN_BIT = 0x80000000
_UNDERFLOW_SUM = 120.0


def _neg_abs(x):
    bits = lax.bitcast_convert_type(x, jnp.uint32) | jnp.uint32(_SIGN_BIT)
    return lax.bitcast_convert_type(bits, F32)


def _sb_attn_kernel(q_ref, k_ref, v_ref, o_ref, acc_ref, carry_ref, *, tq, dh, n_grp):
    i = pl.program_id(2)
    tk, sb = _KEY_CHUNK, _SUFFIX_BLOCK
    n_diag = tq // tk
    row = lax.broadcasted_iota(jnp.int32, (2 * sb, 2 * sb), 0)
    col = lax.broadcasted_iota(jnp.int32, (2 * sb, 2 * sb), 1)
    suffix_op = (((row % sb) >= col) | (col >= sb)).astype(BF16)
    qpos = i * tq + lax.broadcasted_iota(jnp.int32, (tq, tk), 0)
    kcol = lax.broadcasted_iota(jnp.int32, (tq, tk), 1)

    carry_ref[...] = jnp.zeros(carry_ref.shape, F32)
    acc_ref[...] = jnp.zeros(acc_ref.shape, F32)

    def chunk(kb, masked):
        start = pl.multiple_of(kb * tk, tk)
        if masked:
            causal = (start + kcol) < qpos
        for g in range(n_grp):
            lanes = slice(g * dh, (g + 1) * dh)
            q = q_ref[:, lanes]
            k = k_ref[pl.ds(start, tk), lanes]
            v = v_ref[pl.ds(start, tk), lanes]
            z = lax.dot_general(q, k, (((1,), (1,)), ((), ())), preferred_element_type=F32)
            p = jnp.maximum(z, 0.0) + jnp.log(1.0 + jnp.exp(_neg_abs(z)))
            if masked:
                p = jnp.where(causal, p, 0.0)
            hi = p.astype(BF16)
            lo = (p - hi.astype(F32)).astype(BF16)
            tail = carry_ref[g]
            logits = [None] * (tk // sb)
            for blk in reversed(range(tk // sb)):
                cols = slice(blk * sb, (blk + 1) * sb)
                stacked = jnp.concatenate([hi[:, cols], lo[:, cols]], axis=1)
                sums = jnp.dot(stacked, suffix_op, preferred_element_type=F32)
                logits[blk] = (z[:, cols] - tail) - sums[:, :sb]
                tail = tail + sums[:, sb:]
            a = jnp.exp(jnp.concatenate(logits, axis=1))
            if masked:
                a = jnp.where(causal, a, 0.0)
            acc_ref[g] += jnp.dot(a.astype(BF16), v, preferred_element_type=F32)
            carry_ref[g] = tail

    for dg in range(n_diag):
        chunk(i * n_diag + (n_diag - 1 - dg), True)

    n_off = i * n_diag

    def keep_going(state):
        n, live = state
        return jnp.logical_and(n < n_off, live)

    def body(state):
        n, _ = state
        chunk(n_off - 1 - n, False)
        return n + 1, jnp.min(carry_ref[...]) < _UNDERFLOW_SUM

    lax.while_loop(keep_going, body, (jnp.int32(0), jnp.min(carry_ref[...]) < _UNDERFLOW_SUM))
    for g in range(n_grp):
        o_ref[:, g * dh:(g + 1) * dh] = acc_ref[g].astype(o_ref.dtype)


def _sb_attention(qkv, *, tq=512):
    bsz, t_len, d3 = qkv.shape
    d = d3 // 3
    dh = d // N_HEADS
    n_grp = _HEADS_PER_STEP
    wblk = n_grp * dh
    nblk = d // wblk
    kern = functools.partial(_sb_attn_kernel, tq=tq, dh=dh, n_grp=n_grp)
    return pl.pallas_call(
        kern,
        grid=(bsz, nblk, t_len // tq),
        in_specs=[
            pl.BlockSpec((None, tq, wblk), lambda b_, h, i: (b_, i, h)),
            pl.BlockSpec((None, t_len, wblk), lambda b_, h, i: (b_, 0, nblk + h)),
            pl.BlockSpec((None, t_len, wblk), lambda b_, h, i: (b_, 0, 2 * nblk + h)),
        ],
        out_specs=pl.BlockSpec((None, tq, wblk), lambda b_, h, i: (b_, i, h)),
        out_shape=jax.ShapeDtypeStruct((bsz, t_len, d), BF16),
        scratch_shapes=[
            pltpu.VMEM((n_grp, tq, dh), F32),
            pltpu.VMEM((n_grp, tq, _SUFFIX_BLOCK), F32),
        ],
        compiler_params=_params(3),
        name="stick_breaking_attention",
    )(qkv, qkv, qkv)


def _final_norm_kernel(x_ref, g_ref, o_ref):
    x = x_ref[...]
    ms = jnp.mean(x * x, axis=-1, keepdims=True)
    o_ref[...] = x * lax.rsqrt(ms + RMS_EPS) * g_ref[...]


def _final_norm(x, g, *, tm=512):
    bsz, t_len, d = x.shape
    return pl.pallas_call(
        _final_norm_kernel,
        grid=(bsz, t_len // tm),
        in_specs=[
            pl.BlockSpec((None, tm, d), lambda b_, i: (b_, i, 0)),
            pl.BlockSpec((1, d), lambda b_, i: (0, 0)),
        ],
        out_specs=pl.BlockSpec((None, tm, d), lambda b_, i: (b_, i, 0)),
        out_shape=jax.ShapeDtypeStruct(x.shape, F32),
        compiler_params=_params(2),
        name="final_rms_norm",
    )(x, g.reshape(1, d))


def kernel(x, c, mix_norm_g, mix_mod_w, mix_mod_b, cv_pw1_w, cv_pw1_b, cv_dw_w, cv_dw_b, cv_ln_g, cv_ln_b, cv_pw2_w, cv_pw2_b, sb_qkv_w, sb_o_w, ffn_norm_g, ffn_mod_w, ffn_mod_b, ffn_up_w, ffn_dw_w, ffn_dw_b, ffn_down_w, final_norm_g):
    bsz, t_len, d = x.shape
    depth = mix_norm_g.shape[0]
    c_pad = jnp.pad(c, ((0, (-bsz) % _SUBLANES), (0, 0)))
    mix_mod = _adaln(c_pad, mix_mod_w, mix_mod_b)[:, :bsz].reshape(depth, bsz, 1, 3 * d)
    ffn_mod = _adaln(c_pad, ffn_mod_w, ffn_mod_b)[:, :bsz].reshape(depth, bsz, 1, 3 * d)
    zero_bias = jnp.zeros((d,), F32)

    for i in range(depth):
        j = i // 2
        if i % 2 == 0:
            conv = _conformer_in(x, mix_norm_g[i], mix_mod[i], cv_pw1_w[j].astype(BF16), cv_pw1_b[j],
                                 cv_dw_w[j], cv_dw_b[j])
            x = _proj_residual(conv, cv_pw2_w[j].astype(BF16), cv_pw2_b[j], x, mix_mod[i],
                               ln=(cv_ln_g[j], cv_ln_b[j]))
        else:
            qkv = _norm_proj(x, mix_norm_g[i], mix_mod[i], sb_qkv_w[j].astype(BF16))
            o = _sb_attention(qkv)
            x = _proj_residual(o, sb_o_w[j].astype(BF16), zero_bias, x, mix_mod[i], tn=1024)
        act = _ffn_up(x, ffn_norm_g[i], ffn_mod[i], ffn_up_w[i].astype(BF16), ffn_dw_w[i], ffn_dw_b[i])
        x = _proj_residual(act, ffn_down_w[i].astype(BF16), zero_bias, x, ffn_mod[i])
    return _final_norm(x, final_norm_g)
```

```python
import functools
import math

import jax
import jax.numpy as jnp
from jax import lax
from jax.experimental import pallas as pl
from jax.experimental.pallas import tpu as pltpu

N_HEADS = 16
RMS_EPS = 1e-6
LN_EPS = 1e-5

F32 = jnp.float32
BF16 = jnp.bfloat16

_VMEM_LIMIT_BYTES = 56 * 1024 * 1024
_SUBLANES = 8
_LANES = 128
_CONV_ROW_BLOCK = 64


def _params(n_axes):
    return pltpu.CompilerParams(
        dimension_semantics=("arbitrary",) * n_axes,
        vmem_limit_bytes=_VMEM_LIMIT_BYTES)


def _rms_mod(x, g, shift, scale):
    ms = jnp.mean(x * x, axis=-1, keepdims=True)
    return (x * lax.rsqrt(ms + RMS_EPS)) * (g * (1.0 + scale)) + shift


def _silu(x):
    return x * jax.nn.sigmoid(x)


def _adaln_kernel(c_ref, w_ref, b_ref, o_ref):
    c = c_ref[...]
    s = _silu(c)
    o_ref[...] = jnp.dot(s, w_ref[...], preferred_element_type=F32,
                         precision=lax.Precision.HIGHEST) + b_ref[...]


def _adaln(c_pad, w, b, *, tn=512):
    n_layers, d, d3 = w.shape
    bp = c_pad.shape[0]
    return pl.pallas_call(
        _adaln_kernel,
        grid=(n_layers, d3 // tn),
        in_specs=[
            pl.BlockSpec((bp, d), lambda l, j: (0, 0)),
            pl.BlockSpec((None, d, tn), lambda l, j: (l, 0, j)),
            pl.BlockSpec((None, 1, tn), lambda l, j: (l, 0, j)),
        ],
        out_specs=pl.BlockSpec((None, bp, tn), lambda l, j: (l, 0, j)),
        out_shape=jax.ShapeDtypeStruct((n_layers, bp, d3), F32),
        compiler_params=_params(2),
        name="adaln_mod",
    )(c_pad, w, b.reshape(n_layers, 1, d3))


_ROW_STRIDE = 2


def _stage_slab(ubuf_ref, slab, halo, u, *, pad, tm):
    slab_ref = ubuf_ref.at[slab]
    slab_ref[pl.ds(0, pad, stride=_ROW_STRIDE), :] = halo
    slab_ref[pl.ds(_ROW_STRIDE * pad, tm, stride=_ROW_STRIDE), :] = u


def _dwconv_block(ubuf_ref, slab, r0, w_ref, lanes, *, pad, width, rows):
    acc = None
    slab_ref = ubuf_ref.at[slab]
    for k in range(width):
        start = _ROW_STRIDE * (pad + r0 - (width - 1) + k)
        term = w_ref[k:k + 1, lanes] * slab_ref[pl.ds(start, rows, stride=_ROW_STRIDE), :]
        acc = term if acc is None else acc + term
    return acc


def _pw1_kernel(x_ref, g_ref, shift_ref, scale_ref, wv_ref, wg_ref, bv_ref, bg_ref,
                dww_ref, dwb_ref, o_ref, h_ref, carry_ref, ubuf_ref, *, tm, tn_sub, pad, width):
    i = pl.program_id(1)
    j = pl.program_id(2)

    @pl.when(j == 0)
    def _():
        h_ref[...] = _rms_mod(x_ref[...], g_ref[...], shift_ref[...], scale_ref[...]).astype(BF16)

        @pl.when(i == 0)
        def _():
            carry_ref[...] = jnp.zeros(carry_ref.shape, F32)

    h = h_ref[...]
    rb = _CONV_ROW_BLOCK
    def project(c0):
        cols = slice(c0, c0 + tn_sub)
        val = jnp.dot(h, wv_ref[:, cols], preferred_element_type=F32) + bv_ref[:, cols]
        gt = jnp.dot(h, wg_ref[:, cols], preferred_element_type=F32) + bg_ref[:, cols]
        u = val * jax.nn.sigmoid(gt)
        halo = carry_ref[j, :, cols]
        carry_ref[j, :, cols] = u[tm - pad:tm, :]
        for s in range(tn_sub // _LANES):
            _stage_slab(ubuf_ref, c0 // _LANES + s, halo[:, s * _LANES:(s + 1) * _LANES],
                        u[:, s * _LANES:(s + 1) * _LANES], pad=pad, tm=tm)

    def convolve(c0):
        for s in range(tn_sub // _LANES):
            slab = c0 // _LANES + s
            lanes = slice(c0 + s * _LANES, c0 + (s + 1) * _LANES)
            for r0 in range(0, tm, rb):
                y = _dwconv_block(ubuf_ref, slab, r0, dww_ref, lanes, pad=pad, width=width, rows=rb)
                o_ref[r0:r0 + rb, lanes] = y + dwb_ref[:, lanes]

    starts = list(range(0, o_ref.shape[1], tn_sub))
    project(starts[0])
    for n, c0 in enumerate(starts):
        if n + 1 < len(starts):
            project(starts[n + 1])
        convolve(c0)


def _conformer_in(x, g, mod, w_bf, b, dw_w, dw_b, *, tm=512, tn=1024, tn_sub=256):
    bsz, t_len, d = x.shape
    width = dw_w.shape[0]
    pad = -(-(width - 1) // _SUBLANES) * _SUBLANES
    nj = d // tn
    kern = functools.partial(_pw1_kernel, tm=tm, tn_sub=tn_sub, pad=pad, width=width)
    return pl.pallas_call(
        kern,
        grid=(bsz, t_len // tm, nj),
        in_specs=[
            pl.BlockSpec((None, tm, d), lambda b_, i, j: (b_, i, 0)),
            pl.BlockSpec((1, d), lambda b_, i, j: (0, 0)),
            pl.BlockSpec((None, 1, d), lambda b_, i, j: (b_, 0, 0)),
            pl.BlockSpec((None, 1, d), lambda b_, i, j: (b_, 0, 1)),
            pl.BlockSpec((d, tn), lambda b_, i, j: (0, j)),
            pl.BlockSpec((d, tn), lambda b_, i, j: (0, nj + j)),
            pl.BlockSpec((1, tn), lambda b_, i, j: (0, j)),
            pl.BlockSpec((1, tn), lambda b_, i, j: (0, nj + j)),
            pl.BlockSpec((width, tn), lambda b_, i, j: (0, j)),
            pl.BlockSpec((1, tn), lambda b_, i, j: (0, j)),
        ],
        out_specs=pl.BlockSpec((None, tm, tn), lambda b_, i, j: (b_, i, j)),
        out_shape=jax.ShapeDtypeStruct((bsz, t_len, d), F32),
        scratch_shapes=[
            pltpu.VMEM((tm, d), BF16),
            pltpu.VMEM((nj, pad, tn), F32),
            pltpu.VMEM((tn // _LANES, _ROW_STRIDE * (pad + tm), _LANES), F32),
        ],
        compiler_params=_params(3),
        name="conformer_pw1_glu_conv",
    )(x, g.reshape(1, d), mod, mod, w_bf, w_bf, b.reshape(1, -1), b.reshape(1, -1),
      dw_w, dw_b.reshape(1, d))


def _proj_res_kernel(a_ref, w_ref, b_ref, x_ref, gate_ref, o_ref):
    y = jnp.dot(a_ref[...], w_ref[...], preferred_element_type=F32) + b_ref[...]
    o_ref[...] = x_ref[...] + gate_ref[...] * y


def _ln_proj_res_kernel(c_ref, lg_ref, lb_ref, w_ref, b_ref, x_ref, gate_ref, o_ref, h_ref):
    j = pl.program_id(2)

    @pl.when(j == 0)
    def _():
        c = c_ref[...]
        mu = jnp.mean(c, axis=-1, keepdims=True)
        cc = c - mu
        var = jnp.mean(cc * cc, axis=-1, keepdims=True)
        y = cc * lax.rsqrt(var + LN_EPS) * lg_ref[...] + lb_ref[...]
        h_ref[...] = _silu(y).astype(BF16)

    y = jnp.dot(h_ref[...], w_ref[...], preferred_element_type=F32) + b_ref[...]
    o_ref[...] = x_ref[...] + gate_ref[...] * y


def _proj_residual(a, w_bf, b, x, mod, *, ln=None, tm=1024, tn=512):
    bsz, t_len, kdim = a.shape
    d = x.shape[-1]
    nj = d // tn
    gate_blk = 2 * nj
    common = [
        pl.BlockSpec((kdim, tn), lambda b_, i, j: (0, j)),
        pl.BlockSpec((1, tn), lambda b_, i, j: (0, j)),
        pl.BlockSpec((None, tm, tn), lambda b_, i, j: (b_, i, j)),
        pl.BlockSpec((None, 1, tn), lambda b_, i, j: (b_, 0, gate_blk + j)),
    ]
    a_spec = pl.BlockSpec((None, tm, kdim), lambda b_, i, j: (b_, i, 0))
    if ln is None:
        kern, in_specs, args, scratch = _proj_res_kernel, [a_spec] + common, (a,), []
        name = "proj_residual"
    else:
        row = pl.BlockSpec((1, kdim), lambda b_, i, j: (0, 0))
        kern, in_specs = _ln_proj_res_kernel, [a_spec, row, row] + common
        args = (a, ln[0].reshape(1, kdim), ln[1].reshape(1, kdim))
        scratch = [pltpu.VMEM((tm, kdim), BF16)]
        name = "ln_silu_proj_residual"
    return pl.pallas_call(
        kern,
        grid=(bsz, t_len // tm, nj),
        in_specs=in_specs,
        out_specs=pl.BlockSpec((None, tm, tn), lambda b_, i, j: (b_, i, j)),
        out_shape=jax.ShapeDtypeStruct(x.shape, F32),
        scratch_shapes=scratch,
        compiler_params=_params(3),
        name=name,
    )(*args, w_bf, b.reshape(1, d), x, mod)


def _ffn_up_kernel(x_ref, g_ref, shift_ref, scale_ref, wg_ref, wv_ref, dwg_ref, dwv_ref,
                   bg_ref, bv_ref, o_ref, h_ref, carry_ref, ubuf_ref, *, tm, tn_sub, pad, width,
                   n_i, n_j, n_tiles):
    s = pl.program_id(0)
    cur = jnp.minimum(s, n_tiles - 1)
    j = cur % n_j
    i = (cur // n_j) % n_i
    slot = s % 2

    @pl.when(jnp.logical_and(j == 0, s < n_tiles))
    def _():
        h_ref[...] = _rms_mod(x_ref[...], g_ref[...], shift_ref[...], scale_ref[...]).astype(BF16)

        @pl.when(i == 0)
        def _():
            carry_ref[...] = jnp.zeros(carry_ref.shape, F32)

    @pl.when(s == 0)
    def _():
        ubuf_ref[1] = jnp.zeros(ubuf_ref.shape[1:], F32)

    rb = _CONV_ROW_BLOCK
    tn = o_ref.shape[1]

    def step(staged, staging):
        for slab in range(tn // _LANES):
            lanes = slice(slab * _LANES, (slab + 1) * _LANES)
            for r0 in range(0, tm, rb):
                gt, val = [
                    _dwconv_block(staged.at[half], slab, r0, dw_ref, lanes, pad=pad, width=width, rows=rb)
                    + b_ref[:, lanes]
                    for half, (dw_ref, b_ref) in enumerate(((dwg_ref, bg_ref), (dwv_ref, bv_ref)))]
                o_ref[r0:r0 + rb, lanes] = (_silu(gt) * val).astype(o_ref.dtype)

        h = h_ref[...]
        for c0 in range(0, tn, tn_sub):
            cols = slice(c0, c0 + tn_sub)
            for half, w_ref in enumerate((wg_ref, wv_ref)):
                u = jnp.dot(h, w_ref[:, cols], preferred_element_type=F32)
                halo = carry_ref[half, j, :, cols]
                carry_ref[half, j, :, cols] = u[tm - pad:tm, :]
                for k in range(tn_sub // _LANES):
                    _stage_slab(staging.at[half], c0 // _LANES + k, halo[:, k * _LANES:(k + 1) * _LANES],
                                u[:, k * _LANES:(k + 1) * _LANES], pad=pad, tm=tm)

    for parity in range(2):
        @pl.when(slot == parity)
        def _():
            step(ubuf_ref.at[1 - parity], ubuf_ref.at[parity])


def _ffn_up(x, g, mod, w_bf, dw_w, dw_b, *, tm=1024, tn=512, tn_sub=256):
    bsz, t_len, d = x.shape
    f2 = w_bf.shape[1]
    f = f2 // 2
    width = dw_w.shape[0]
    pad = -(-(width - 1) // _SUBLANES) * _SUBLANES
    n_i, n_j = t_len // tm, f // tn
    n_tiles = bsz * n_i * n_j
    kern = functools.partial(_ffn_up_kernel, tm=tm, tn_sub=tn_sub, pad=pad, width=width,
                             n_i=n_i, n_j=n_j, n_tiles=n_tiles)

    def tile(t):
        return t // (n_i * n_j), (t // n_j) % n_i, t % n_j

    def cur(s):
        return tile(jnp.minimum(s, n_tiles - 1))

    def prev(s):
        return tile(jnp.maximum(s - 1, 0))

    return pl.pallas_call(
        kern,
        grid=(n_tiles + 1,),
        in_specs=[
            pl.BlockSpec((None, tm, d), lambda s: (cur(s)[0], cur(s)[1], 0)),
            pl.BlockSpec((1, d), lambda s: (0, 0)),
            pl.BlockSpec((None, 1, d), lambda s: (cur(s)[0], 0, 0)),
            pl.BlockSpec((None, 1, d), lambda s: (cur(s)[0], 0, 1)),
            pl.BlockSpec((d, tn), lambda s: (0, cur(s)[2])),
            pl.BlockSpec((d, tn), lambda s: (0, n_j + cur(s)[2])),
            pl.BlockSpec((width, tn), lambda s: (0, prev(s)[2])),
            pl.BlockSpec((width, tn), lambda s: (0, n_j + prev(s)[2])),
            pl.BlockSpec((1, tn), lambda s: (0, prev(s)[2])),
            pl.BlockSpec((1, tn), lambda s: (0, n_j + prev(s)[2])),
        ],
        out_specs=pl.BlockSpec((None, tm, tn), lambda s: prev(s)),
        out_shape=jax.ShapeDtypeStruct((bsz, t_len, f), BF16),
        scratch_shapes=[
            pltpu.VMEM((tm, d), BF16),
            pltpu.VMEM((2, n_j, pad, tn), F32),
            pltpu.VMEM((2, 2, tn // _LANES, _ROW_STRIDE * (pad + tm), _LANES), F32),
        ],
        compiler_params=_params(1),
        name="ffn_up_conv_swiglu",
    )(x, g.reshape(1, d), mod, mod, w_bf, w_bf, dw_w, dw_w, dw_b.reshape(1, f2), dw_b.reshape(1, f2))


def _norm_proj_kernel(x_ref, g_ref, shift_ref, scale_ref, w_ref, o_ref, h_ref, *, q_tiles, q_scale):
    j = pl.program_id(2)

    @pl.when(j == 0)
    def _():
        h_ref[...] = _rms_mod(x_ref[...], g_ref[...], shift_ref[...], scale_ref[...]).astype(BF16)

    y = jnp.dot(h_ref[...], w_ref[...], preferred_element_type=F32)
    o_ref[...] = (y * jnp.where(j < q_tiles, q_scale, 1.0)).astype(o_ref.dtype)


def _norm_proj(x, g, mod, w_bf, *, tm=1024, tn=1024):
    bsz, t_len, d = x.shape
    n_out = w_bf.shape[1]
    kern = functools.partial(_norm_proj_kernel, q_tiles=d // tn,
                             q_scale=1.0 / math.sqrt(d // N_HEADS))
    return pl.pallas_call(
        kern,
        grid=(bsz, t_len // tm, n_out // tn),
        in_specs=[
            pl.BlockSpec((None, tm, d), lambda b_, i, j: (b_, i, 0)),
            pl.BlockSpec((1, d), lambda b_, i, j: (0, 0)),
            pl.BlockSpec((None, 1, d), lambda b_, i, j: (b_, 0, 0)),
            pl.BlockSpec((None, 1, d), lambda b_, i, j: (b_, 0, 1)),
            pl.BlockSpec((d, tn), lambda b_, i, j: (0, j)),
        ],
        out_specs=pl.BlockSpec((None, tm, tn), lambda b_, i, j: (b_, i, j)),
        out_shape=jax.ShapeDtypeStruct((bsz, t_len, n_out), BF16),
        scratch_shapes=[pltpu.VMEM((tm, d), BF16)],
        compiler_params=_params(3),
        name="norm_qkv_proj",
    )(x, g.reshape(1, d), mod, mod, w_bf)


_HEADS_PER_STEP = 4
_KEY_CHUNK = 256
_SUFFIX_BLOCK = 128
_SIGN_BIT = 0x80000000
_UNDERFLOW_SUM = 120.0


def _neg_abs(x):
    bits = lax.bitcast_convert_type(x, jnp.uint32) | jnp.uint32(_SIGN_BIT)
    return lax.bitcast_convert_type(bits, F32)


def _sb_attn_kernel(q_ref, k_ref, v_ref, o_ref, acc_ref, carry_ref, *, tq, dh, n_grp):
    i = pl.program_id(2)
    tk, sb = _KEY_CHUNK, _SUFFIX_BLOCK
    n_diag = tq // tk
    row = lax.broadcasted_iota(jnp.int32, (2 * sb, 2 * sb), 0)
    col = lax.broadcasted_iota(jnp.int32, (2 * sb, 2 * sb), 1)
    suffix_op = (((row % sb) >= col) | (col >= sb)).astype(BF16)
    qpos = i * tq + lax.broadcasted_iota(jnp.int32, (tq, tk), 0)
    kcol = lax.broadcasted_iota(jnp.int32, (tq, tk), 1)

    carry_ref[...] = jnp.zeros(carry_ref.shape, F32)
    acc_ref[...] = jnp.zeros(acc_ref.shape, F32)

    def chunk(kb, masked, row0=0):
        start = pl.multiple_of(kb * tk, tk)
        if masked:
            causal = ((start + kcol) < qpos)[row0:]
        for g in range(n_grp):
            lanes = slice(g * dh, (g + 1) * dh)
            q = q_ref[row0:, lanes]
            k = k_ref[pl.ds(start, tk), lanes]
            v = v_ref[pl.ds(start, tk), lanes]
            z = lax.dot_general(q, k, (((1,), (1,)), ((), ())), preferred_element_type=F32)
            p = jnp.maximum(z, 0.0) + jnp.log(1.0 + jnp.exp(_neg_abs(z)))
            if masked:
                p = jnp.where(causal, p, 0.0)
            hi = p.astype(BF16)
            lo = (p - hi.astype(F32)).astype(BF16)
            tail = carry_ref[g, row0:]
            logits = [None] * (tk // sb)
            for blk in reversed(range(tk // sb)):
                cols = slice(blk * sb, (blk + 1) * sb)
                stacked = jnp.concatenate([hi[:, cols], lo[:, cols]], axis=1)
                sums = jnp.dot(stacked, suffix_op, preferred_element_type=F32)
                logits[blk] = (z[:, cols] - tail) - sums[:, :sb]
                tail = tail + sums[:, sb:]
            a = jnp.exp(jnp.concatenate(logits, axis=1))
            if masked:
                a = jnp.where(causal, a, 0.0)
            acc_ref[g, row0:] += jnp.dot(a.astype(BF16), v, preferred_element_type=F32)
            carry_ref[g, row0:] = tail

    for dg in range(n_diag):
        chunk(i * n_diag + (n_diag - 1 - dg), True, row0=(n_diag - 1 - dg) * tk)

    n_off = i * n_diag

    def keep_going(state):
        n, live = state
        return jnp.logical_and(n < n_off, live)

    def body(state):
        n, _ = state
        chunk(n_off - 1 - n, False)
        return n + 1, jnp.min(carry_ref[...]) < _UNDERFLOW_SUM

    lax.while_loop(keep_going, body, (jnp.int32(0), jnp.min(carry_ref[...]) < _UNDERFLOW_SUM))
    for g in range(n_grp):
        o_ref[:, g * dh:(g + 1) * dh] = acc_ref[g].astype(o_ref.dtype)


def _sb_attention(qkv, *, tq=512):
    bsz, t_len, d3 = qkv.shape
    d = d3 // 3
    dh = d // N_HEADS
    n_grp = _HEADS_PER_STEP
    wblk = n_grp * dh
    nblk = d // wblk
    kern = functools.partial(_sb_attn_kernel, tq=tq, dh=dh, n_grp=n_grp)
    return pl.pallas_call(
        kern,
        grid=(bsz, nblk, t_len // tq),
        in_specs=[
            pl.BlockSpec((None, tq, wblk), lambda b_, h, i: (b_, i, h)),
            pl.BlockSpec((None, t_len, wblk), lambda b_, h, i: (b_, 0, nblk + h)),
            pl.BlockSpec((None, t_len, wblk), lambda b_, h, i: (b_, 0, 2 * nblk + h)),
        ],
        out_specs=pl.BlockSpec((None, tq, wblk), lambda b_, h, i: (b_, i, h)),
        out_shape=jax.ShapeDtypeStruct((bsz, t_len, d), BF16),
        scratch_shapes=[
            pltpu.VMEM((n_grp, tq, dh), F32),
            pltpu.VMEM((n_grp, tq, _SUFFIX_BLOCK), F32),
        ],
        compiler_params=_params(3),
        name="stick_breaking_attention",
    )(qkv, qkv, qkv)


def _final_norm_kernel(x_ref, g_ref, o_ref):
    x = x_ref[...]
    ms = jnp.mean(x * x, axis=-1, keepdims=True)
    o_ref[...] = x * lax.rsqrt(ms + RMS_EPS) * g_ref[...]


def _final_norm(x, g, *, tm=512):
    bsz, t_len, d = x.shape
    return pl.pallas_call(
        _final_norm_kernel,
        grid=(bsz, t_len // tm),
        in_specs=[
            pl.BlockSpec((None, tm, d), lambda b_, i: (b_, i, 0)),
            pl.BlockSpec((1, d), lambda b_, i: (0, 0)),
        ],
        out_specs=pl.BlockSpec((None, tm, d), lambda b_, i: (b_, i, 0)),
        out_shape=jax.ShapeDtypeStruct(x.shape, F32),
        compiler_params=_params(2),
        name="final_rms_norm",
    )(x, g.reshape(1, d))


def kernel(x, c, mix_norm_g, mix_mod_w, mix_mod_b, cv_pw1_w, cv_pw1_b, cv_dw_w, cv_dw_b, cv_ln_g, cv_ln_b, cv_pw2_w, cv_pw2_b, sb_qkv_w, sb_o_w, ffn_norm_g, ffn_mod_w, ffn_mod_b, ffn_up_w, ffn_dw_w, ffn_dw_b, ffn_down_w, final_norm_g):
    bsz, t_len, d = x.shape
    depth = mix_norm_g.shape[0]
    c_pad = jnp.pad(c, ((0, (-bsz) % _SUBLANES), (0, 0)))
    mix_mod = _adaln(c_pad, mix_mod_w, mix_mod_b)[:, :bsz].reshape(depth, bsz, 1, 3 * d)
    ffn_mod = _adaln(c_pad, ffn_mod_w, ffn_mod_b)[:, :bsz].reshape(depth, bsz, 1, 3 * d)
    zero_bias = jnp.zeros((d,), F32)

    for i in range(depth):
        j = i // 2
        if i % 2 == 0:
            conv = _conformer_in(x, mix_norm_g[i], mix_mod[i], cv_pw1_w[j].astype(BF16), cv_pw1_b[j],
                                 cv_dw_w[j], cv_dw_b[j])
            x = _proj_residual(conv, cv_pw2_w[j].astype(BF16), cv_pw2_b[j], x, mix_mod[i],
                               ln=(cv_ln_g[j], cv_ln_b[j]))
        else:
            qkv = _norm_proj(x, mix_norm_g[i], mix_mod[i], sb_qkv_w[j].astype(BF16))
            o = _sb_attention(qkv)
            x = _proj_residual(o, sb_o_w[j].astype(BF16), zero_bias, x, mix_mod[i], tn=1024)
        act = _ffn_up(x, ffn_norm_g[i], ffn_mod[i], ffn_up_w[i].astype(BF16), ffn_dw_w[i], ffn_dw_b[i])
        x = _proj_residual(act, ffn_down_w[i].astype(BF16), zero_bias, x, ffn_mod[i])
    return _final_norm(x, final_norm_g)
```

```python
import functools
import math

import jax
import jax.numpy as jnp
from jax import lax
from jax.experimental import pallas as pl
from jax.experimental.pallas import tpu as pltpu

N_HEADS = 16
RMS_EPS = 1e-6
LN_EPS = 1e-5

F32 = jnp.float32
BF16 = jnp.bfloat16

_VMEM_LIMIT_BYTES = 56 * 1024 * 1024
_SUBLANES = 8
_LANES = 128
_CONV_ROW_BLOCK = 64


def _params(n_axes):
    return pltpu.CompilerParams(
        dimension_semantics=("arbitrary",) * n_axes,
        vmem_limit_bytes=_VMEM_LIMIT_BYTES)


def _rms_mod(x, g, shift, scale):
    ms = jnp.mean(x * x, axis=-1, keepdims=True)
    return (x * lax.rsqrt(ms + RMS_EPS)) * (g * (1.0 + scale)) + shift


def _silu(x):
    return x * jax.nn.sigmoid(x)


def _adaln_kernel(c_ref, w_ref, b_ref, o_ref):
    c = c_ref[...]
    s = _silu(c)
    o_ref[...] = jnp.dot(s, w_ref[...], preferred_element_type=F32,
                         precision=lax.Precision.HIGHEST) + b_ref[...]


def _adaln(c_pad, w, b, *, tn=512):
    n_layers, d, d3 = w.shape
    bp = c_pad.shape[0]
    return pl.pallas_call(
        _adaln_kernel,
        grid=(n_layers, d3 // tn),
        in_specs=[
            pl.BlockSpec((bp, d), lambda l, j: (0, 0)),
            pl.BlockSpec((None, d, tn), lambda l, j: (l, 0, j)),
            pl.BlockSpec((None, 1, tn), lambda l, j: (l, 0, j)),
        ],
        out_specs=pl.BlockSpec((None, bp, tn), lambda l, j: (l, 0, j)),
        out_shape=jax.ShapeDtypeStruct((n_layers, bp, d3), F32),
        compiler_params=_params(2),
        name="adaln_mod",
    )(c_pad, w, b.reshape(n_layers, 1, d3))


_ROW_STRIDE = 2


def _stage_rows(slab_ref, t0, rows):
    slab_ref[pl.ds(_ROW_STRIDE * t0, rows.shape[0], stride=_ROW_STRIDE), :] = rows


def _stage_slab(ubuf_ref, slab, halo, u, *, pad, tm):
    _stage_rows(ubuf_ref.at[slab], 0, halo)
    _stage_rows(ubuf_ref.at[slab], pad, u)


def _dwconv_block(ubuf_ref, slab, r0, w_ref, lanes, *, pad, width, rows):
    acc = None
    slab_ref = ubuf_ref.at[slab]
    for k in range(width):
        start = _ROW_STRIDE * (pad + r0 - (width - 1) + k)
        term = w_ref[k:k + 1, lanes] * slab_ref[pl.ds(start, rows, stride=_ROW_STRIDE), :]
        acc = term if acc is None else acc + term
    return acc


def _pw1_kernel(x_ref, g_ref, shift_ref, scale_ref, wv_ref, wg_ref, bv_ref, bg_ref,
                dww_ref, dwb_ref, o_ref, h_ref, carry_ref, ubuf_ref, *, tm, tn_sub, pad, width):
    i = pl.program_id(1)
    j = pl.program_id(2)

    @pl.when(j == 0)
    def _():
        h_ref[...] = _rms_mod(x_ref[...], g_ref[...], shift_ref[...], scale_ref[...]).astype(BF16)

        @pl.when(i == 0)
        def _():
            carry_ref[...] = jnp.zeros(carry_ref.shape, F32)

    h = h_ref[...]
    rb = _CONV_ROW_BLOCK
    def project(c0):
        cols = slice(c0, c0 + tn_sub)
        val = jnp.dot(h, wv_ref[:, cols], preferred_element_type=F32) + bv_ref[:, cols]
        gt = jnp.dot(h, wg_ref[:, cols], preferred_element_type=F32) + bg_ref[:, cols]
        u = val * jax.nn.sigmoid(gt)
        halo = carry_ref[j, :, cols]
        carry_ref[j, :, cols] = u[tm - pad:tm, :]
        for s in range(tn_sub // _LANES):
            _stage_slab(ubuf_ref, c0 // _LANES + s, halo[:, s * _LANES:(s + 1) * _LANES],
                        u[:, s * _LANES:(s + 1) * _LANES], pad=pad, tm=tm)

    def convolve(c0):
        for s in range(tn_sub // _LANES):
            slab = c0 // _LANES + s
            lanes = slice(c0 + s * _LANES, c0 + (s + 1) * _LANES)
            for r0 in range(0, tm, rb):
                y = _dwconv_block(ubuf_ref, slab, r0, dww_ref, lanes, pad=pad, width=width, rows=rb)
                o_ref[r0:r0 + rb, lanes] = y + dwb_ref[:, lanes]

    starts = list(range(0, o_ref.shape[1], tn_sub))
    project(starts[0])
    for n, c0 in enumerate(starts):
        if n + 1 < len(starts):
            project(starts[n + 1])
        convolve(c0)


def _conformer_in(x, g, mod, w_bf, b, dw_w, dw_b, *, tm=512, tn=1024, tn_sub=256):
    bsz, t_len, d = x.shape
    width = dw_w.shape[0]
    pad = -(-(width - 1) // _SUBLANES) * _SUBLANES
    nj = d // tn
    kern = functools.partial(_pw1_kernel, tm=tm, tn_sub=tn_sub, pad=pad, width=width)
    return pl.pallas_call(
        kern,
        grid=(bsz, t_len // tm, nj),
        in_specs=[
            pl.BlockSpec((None, tm, d), lambda b_, i, j: (b_, i, 0)),
            pl.BlockSpec((1, d), lambda b_, i, j: (0, 0)),
            pl.BlockSpec((None, 1, d), lambda b_, i, j: (b_, 0, 0)),
            pl.BlockSpec((None, 1, d), lambda b_, i, j: (b_, 0, 1)),
            pl.BlockSpec((d, tn), lambda b_, i, j: (0, j)),
            pl.BlockSpec((d, tn), lambda b_, i, j: (0, nj + j)),
            pl.BlockSpec((1, tn), lambda b_, i, j: (0, j)),
            pl.BlockSpec((1, tn), lambda b_, i, j: (0, nj + j)),
            pl.BlockSpec((width, tn), lambda b_, i, j: (0, j)),
            pl.BlockSpec((1, tn), lambda b_, i, j: (0, j)),
        ],
        out_specs=pl.BlockSpec((None, tm, tn), lambda b_, i, j: (b_, i, j)),
        out_shape=jax.ShapeDtypeStruct((bsz, t_len, d), F32),
        scratch_shapes=[
            pltpu.VMEM((tm, d), BF16),
            pltpu.VMEM((nj, pad, tn), F32),
            pltpu.VMEM((tn // _LANES, _ROW_STRIDE * (pad + tm), _LANES), F32),
        ],
        compiler_params=_params(3),
        name="conformer_pw1_glu_conv",
    )(x, g.reshape(1, d), mod, mod, w_bf, w_bf, b.reshape(1, -1), b.reshape(1, -1),
      dw_w, dw_b.reshape(1, d))


_PROJ_COL_CHUNK = 512


def _proj_res_kernel(a_ref, lg_ref, lb_ref, w_ref, b_ref, x_ref, gate_ref, fg_ref, o_ref, *,
                     pre_ln, post_norm):
    a = a_ref[...]
    if pre_ln:
        mu = jnp.mean(a, axis=-1, keepdims=True)
        ac = a - mu
        var = jnp.mean(ac * ac, axis=-1, keepdims=True)
        a = _silu(ac * lax.rsqrt(var + LN_EPS) * lg_ref[...] + lb_ref[...]).astype(BF16)
    d = o_ref.shape[1]
    for c0 in range(0, d, _PROJ_COL_CHUNK):
        cols = slice(c0, c0 + _PROJ_COL_CHUNK)
        y = jnp.dot(a, w_ref[:, cols], preferred_element_type=F32) + b_ref[:, cols]
        o_ref[:, cols] = x_ref[:, cols] + gate_ref[:, cols] * y
    if post_norm:
        r = o_ref[...]
        ms = jnp.mean(r * r, axis=-1, keepdims=True)
        o_ref[...] = r * lax.rsqrt(ms + RMS_EPS) * fg_ref[...]


def _proj_residual(a, w_bf, b, x, mod, *, ln=None, final_g=None, tm=512):
    bsz, t_len, kdim = a.shape
    d = x.shape[-1]
    pre_ln, post_norm = ln is not None, final_g is not None
    lg, lb = ln if pre_ln else (jnp.zeros((kdim,), F32), jnp.zeros((kdim,), F32))
    fg = final_g if post_norm else jnp.zeros((d,), F32)
    kern = functools.partial(_proj_res_kernel, pre_ln=pre_ln, post_norm=post_norm)
    row_k = pl.BlockSpec((1, kdim), lambda b_, i: (0, 0))
    row_d = pl.BlockSpec((1, d), lambda b_, i: (0, 0))
    tile = pl.BlockSpec((None, tm, d), lambda b_, i: (b_, i, 0))
    return pl.pallas_call(
        kern,
        grid=(bsz, t_len // tm),
        in_specs=[
            pl.BlockSpec((None, tm, kdim), lambda b_, i: (b_, i, 0)),
            row_k, row_k,
            pl.BlockSpec((kdim, d), lambda b_, i: (0, 0), pipeline_mode=pl.Buffered(1)),
            row_d,
            tile,
            pl.BlockSpec((None, 1, d), lambda b_, i: (b_, 0, 2)),
            row_d,
        ],
        out_specs=tile,
        out_shape=jax.ShapeDtypeStruct(x.shape, F32),
        compiler_params=_params(2),
        name="ln_silu_proj_residual" if pre_ln else "proj_residual",
    )(a, lg.reshape(1, kdim), lb.reshape(1, kdim), w_bf, b.reshape(1, d), x, mod, fg.reshape(1, d))


def _ffn_up_kernel(x_ref, g_ref, shift_ref, scale_ref, wg_ref, wv_ref, dwg_ref, dwv_ref,
                   bg_ref, bv_ref, o_ref, h_ref, carry_ref, ubuf_ref, *, tm, tm_sub, tn_sub, pad, width):
    i = pl.program_id(1)
    j = pl.program_id(2)

    @pl.when(j == 0)
    def _():
        h_ref[...] = _rms_mod(x_ref[...], g_ref[...], shift_ref[...], scale_ref[...]).astype(BF16)

        @pl.when(i == 0)
        def _():
            carry_ref[...] = jnp.zeros(carry_ref.shape, F32)

    rb = _CONV_ROW_BLOCK
    sources = ((wg_ref, dwg_ref, bg_ref), (wv_ref, dwv_ref, bv_ref))
    for c0 in range(0, o_ref.shape[1], tn_sub):
        cols = slice(c0, c0 + tn_sub)
        for m0 in range(0, tm, tm_sub):
            for half, (w_ref, _, _) in enumerate(sources):
                u = jnp.dot(h_ref[m0:m0 + tm_sub, :], w_ref[:, cols], preferred_element_type=F32)
                if m0 == 0:
                    halo = carry_ref[half, j, :, cols]
                if m0 + tm_sub == tm:
                    carry_ref[half, j, :, cols] = u[tm_sub - pad:, :]
                for s in range(tn_sub // _LANES):
                    slab_ref = ubuf_ref.at[half].at[c0 // _LANES + s]
                    if m0 == 0:
                        _stage_rows(slab_ref, 0, halo[:, s * _LANES:(s + 1) * _LANES])
                    _stage_rows(slab_ref, pad + m0, u[:, s * _LANES:(s + 1) * _LANES])
            for s in range(tn_sub // _LANES):
                slab = c0 // _LANES + s
                lanes = slice(c0 + s * _LANES, c0 + (s + 1) * _LANES)
                for r0 in range(m0, m0 + tm_sub, rb):
                    gt, val = [
                        _dwconv_block(ubuf_ref.at[half], slab, r0, dw_ref, lanes, pad=pad, width=width, rows=rb)
                        + b_ref[:, lanes]
                        for half, (_, dw_ref, b_ref) in enumerate(sources)]
                    o_ref[r0:r0 + rb, lanes] = (_silu(gt) * val).astype(o_ref.dtype)


def _ffn_up(x, g, mod, w_bf, dw_w, dw_b, *, tm=1024, tm_sub=1024, tn=512, tn_sub=256):
    bsz, t_len, d = x.shape
    f2 = w_bf.shape[1]
    f = f2 // 2
    width = dw_w.shape[0]
    pad = -(-(width - 1) // _SUBLANES) * _SUBLANES
    nj = f // tn
    kern = functools.partial(_ffn_up_kernel, tm=tm, tm_sub=tm_sub, tn_sub=tn_sub, pad=pad, width=width)
    return pl.pallas_call(
        kern,
        grid=(bsz, t_len // tm, nj),
        in_specs=[
            pl.BlockSpec((None, tm, d), lambda b_, i, j: (b_, i, 0)),
            pl.BlockSpec((1, d), lambda b_, i, j: (0, 0)),
            pl.BlockSpec((None, 1, d), lambda b_, i, j: (b_, 0, 0)),
            pl.BlockSpec((None, 1, d), lambda b_, i, j: (b_, 0, 1)),
            pl.BlockSpec((d, tn), lambda b_, i, j: (0, j)),
            pl.BlockSpec((d, tn), lambda b_, i, j: (0, nj + j)),
            pl.BlockSpec((width, tn), lambda b_, i, j: (0, j)),
            pl.BlockSpec((width, tn), lambda b_, i, j: (0, nj + j)),
            pl.BlockSpec((1, tn), lambda b_, i, j: (0, j)),
            pl.BlockSpec((1, tn), lambda b_, i, j: (0, nj + j)),
        ],
        out_specs=pl.BlockSpec((None, tm, tn), lambda b_, i, j: (b_, i, j)),
        out_shape=jax.ShapeDtypeStruct((bsz, t_len, f), BF16),
        scratch_shapes=[
            pltpu.VMEM((tm, d), BF16),
            pltpu.VMEM((2, nj, pad, tn), F32),
            pltpu.VMEM((2, tn // _LANES, _ROW_STRIDE * (pad + tm), _LANES), F32),
        ],
        compiler_params=_params(3),
        name="ffn_up_conv_swiglu",
    )(x, g.reshape(1, d), mod, mod, w_bf, w_bf, dw_w, dw_w, dw_b.reshape(1, f2), dw_b.reshape(1, f2))


def _norm_proj_kernel(x_ref, g_ref, shift_ref, scale_ref, w_ref, o_ref, h_ref, *, q_tiles, q_scale):
    j = pl.program_id(2)

    @pl.when(j == 0)
    def _():
        h_ref[...] = _rms_mod(x_ref[...], g_ref[...], shift_ref[...], scale_ref[...]).astype(BF16)

    y = jnp.dot(h_ref[...], w_ref[...], preferred_element_type=F32)
    o_ref[...] = (y * jnp.where(j < q_tiles, q_scale, 1.0)).astype(o_ref.dtype)


def _norm_proj(x, g, mod, w_bf, *, tm=1024, tn=1024):
    bsz, t_len, d = x.shape
    n_out = w_bf.shape[1]
    kern = functools.partial(_norm_proj_kernel, q_tiles=d // tn,
                             q_scale=1.0 / math.sqrt(d // N_HEADS))
    return pl.pallas_call(
        kern,
        grid=(bsz, t_len // tm, n_out // tn),
        in_specs=[
            pl.BlockSpec((None, tm, d), lambda b_, i, j: (b_, i, 0)),
            pl.BlockSpec((1, d), lambda b_, i, j: (0, 0)),
            pl.BlockSpec((None, 1, d), lambda b_, i, j: (b_, 0, 0)),
            pl.BlockSpec((None, 1, d), lambda b_, i, j: (b_, 0, 1)),
            pl.BlockSpec((d, tn), lambda b_, i, j: (0, j)),
        ],
        out_specs=pl.BlockSpec((None, tm, tn), lambda b_, i, j: (b_, i, j)),
        out_shape=jax.ShapeDtypeStruct((bsz, t_len, n_out), BF16),
        scratch_shapes=[pltpu.VMEM((tm, d), BF16)],
        compiler_params=_params(3),
        name="norm_qkv_proj",
    )(x, g.reshape(1, d), mod, mod, w_bf)


_HEADS_PER_STEP = 4
_KEY_CHUNK = 256
_SUFFIX_BLOCK = 128
_SIGN_BIT = 0x80000000
_UNDERFLOW_SUM = 120.0


def _neg_abs(x):
    bits = lax.bitcast_convert_type(x, jnp.uint32) | jnp.uint32(_SIGN_BIT)
    return lax.bitcast_convert_type(bits, F32)


def _sb_attn_kernel(q_ref, k_ref, v_ref, o_ref, acc_ref, carry_ref, *, tq, dh, n_grp):
    i = pl.program_id(2)
    tk, sb = _KEY_CHUNK, _SUFFIX_BLOCK
    n_diag = tq // tk
    row = lax.broadcasted_iota(jnp.int32, (2 * sb, 2 * sb), 0)
    col = lax.broadcasted_iota(jnp.int32, (2 * sb, 2 * sb), 1)
    suffix_op = (((row % sb) >= col) | (col >= sb)).astype(BF16)
    qpos = i * tq + lax.broadcasted_iota(jnp.int32, (tq, tk), 0)
    kcol = lax.broadcasted_iota(jnp.int32, (tq, tk), 1)

    carry_ref[...] = jnp.zeros(carry_ref.shape, F32)
    acc_ref[...] = jnp.zeros(acc_ref.shape, F32)

    def chunk(kb, masked, row0=0):
        start = pl.multiple_of(kb * tk, tk)
        if masked:
            causal = ((start + kcol) < qpos)[row0:]
        for g in range(n_grp):
            lanes = slice(g * dh, (g + 1) * dh)
            q = q_ref[row0:, lanes]
            k = k_ref[pl.ds(start, tk), lanes]
            v = v_ref[pl.ds(start, tk), lanes]
            z = lax.dot_general(q, k, (((1,), (1,)), ((), ())), preferred_element_type=F32)
            p = jnp.maximum(z, 0.0) + jnp.log(1.0 + jnp.exp(_neg_abs(z)))
            if masked:
                p = jnp.where(causal, p, 0.0)
            hi = p.astype(BF16)
            lo = (p - hi.astype(F32)).astype(BF16)
            tail = carry_ref[g, row0:]
            logits = [None] * (tk // sb)
            for blk in reversed(range(tk // sb)):
                cols = slice(blk * sb, (blk + 1) * sb)
                stacked = jnp.concatenate([hi[:, cols], lo[:, cols]], axis=1)
                sums = jnp.dot(stacked, suffix_op, preferred_element_type=F32)
                logits[blk] = (z[:, cols] - tail) - sums[:, :sb]
                tail = tail + sums[:, sb:]
            a = jnp.exp(jnp.concatenate(logits, axis=1))
            if masked:
                a = jnp.where(causal, a, 0.0)
            acc_ref[g, row0:] += jnp.dot(a.astype(BF16), v, preferred_element_type=F32)
            carry_ref[g, row0:] = tail

    for dg in range(n_diag):
        chunk(i * n_diag + (n_diag - 1 - dg), True, row0=(n_diag - 1 - dg) * tk)

    n_off = i * n_diag

    def keep_going(state):
        n, live = state
        return jnp.logical_and(n < n_off, live)

    def body(state):
        n, _ = state
        chunk(n_off - 1 - n, False)
        return n + 1, jnp.min(carry_ref[...]) < _UNDERFLOW_SUM

    lax.while_loop(keep_going, body, (jnp.int32(0), jnp.min(carry_ref[...]) < _UNDERFLOW_SUM))
    for g in range(n_grp):
        o_ref[:, g * dh:(g + 1) * dh] = acc_ref[g].astype(o_ref.dtype)


def _sb_attention(qkv, *, tq=512):
    bsz, t_len, d3 = qkv.shape
    d = d3 // 3
    dh = d // N_HEADS
    n_grp = _HEADS_PER_STEP
    wblk = n_grp * dh
    nblk = d // wblk
    kern = functools.partial(_sb_attn_kernel, tq=tq, dh=dh, n_grp=n_grp)
    return pl.pallas_call(
        kern,
        grid=(bsz, nblk, t_len // tq),
        in_specs=[
            pl.BlockSpec((None, tq, wblk), lambda b_, h, i: (b_, i, h)),
            pl.BlockSpec((None, t_len, wblk), lambda b_, h, i: (b_, 0, nblk + h)),
            pl.BlockSpec((None, t_len, wblk), lambda b_, h, i: (b_, 0, 2 * nblk + h)),
        ],
        out_specs=pl.BlockSpec((None, tq, wblk), lambda b_, h, i: (b_, i, h)),
        out_shape=jax.ShapeDtypeStruct((bsz, t_len, d), BF16),
        scratch_shapes=[
            pltpu.VMEM((n_grp, tq, dh), F32),
            pltpu.VMEM((n_grp, tq, _SUFFIX_BLOCK), F32),
        ],
        compiler_params=_params(3),
        name="stick_breaking_attention",
    )(qkv, qkv, qkv)


def kernel(x, c, mix_norm_g, mix_mod_w, mix_mod_b, cv_pw1_w, cv_pw1_b, cv_dw_w, cv_dw_b, cv_ln_g, cv_ln_b, cv_pw2_w, cv_pw2_b, sb_qkv_w, sb_o_w, ffn_norm_g, ffn_mod_w, ffn_mod_b, ffn_up_w, ffn_dw_w, ffn_dw_b, ffn_down_w, final_norm_g):
    bsz, t_len, d = x.shape
    depth = mix_norm_g.shape[0]
    c_pad = jnp.pad(c, ((0, (-bsz) % _SUBLANES), (0, 0)))
    mix_mod = _adaln(c_pad, mix_mod_w, mix_mod_b)[:, :bsz].reshape(depth, bsz, 1, 3 * d)
    ffn_mod = _adaln(c_pad, ffn_mod_w, ffn_mod_b)[:, :bsz].reshape(depth, bsz, 1, 3 * d)
    zero_bias = jnp.zeros((d,), F32)

    for i in range(depth):
        j = i // 2
        if i % 2 == 0:
            conv = _conformer_in(x, mix_norm_g[i], mix_mod[i], cv_pw1_w[j].astype(BF16), cv_pw1_b[j],
                                 cv_dw_w[j], cv_dw_b[j])
            x = _proj_residual(conv, cv_pw2_w[j].astype(BF16), cv_pw2_b[j], x, mix_mod[i],
                               ln=(cv_ln_g[j], cv_ln_b[j]))
        else:
            qkv = _norm_proj(x, mix_norm_g[i], mix_mod[i], sb_qkv_w[j].astype(BF16))
            o = _sb_attention(qkv)
            x = _proj_residual(o, sb_o_w[j].astype(BF16), zero_bias, x, mix_mod[i])
        act = _ffn_up(x, ffn_norm_g[i], ffn_mod[i], ffn_up_w[i].astype(BF16), ffn_dw_w[i], ffn_dw_b[i])
        x = _proj_residual(act, ffn_down_w[i].astype(BF16), zero_bias, x, ffn_mod[i],
                           final_g=final_norm_g if i == depth - 1 else None)
    return x
```

```python
import functools
import math

import jax
import jax.numpy as jnp
from jax import lax
from jax.experimental import pallas as pl
from jax.experimental.pallas import tpu as pltpu

N_HEADS = 16
RMS_EPS = 1e-6
LN_EPS = 1e-5

F32 = jnp.float32
BF16 = jnp.bfloat16

_VMEM_LIMIT_BYTES = 56 * 1024 * 1024
_SUBLANES = 8
_LANES = 128
_CONV_ROW_BLOCK = 64


def _params(n_axes):
    return pltpu.CompilerParams(
        dimension_semantics=("arbitrary",) * n_axes,
        vmem_limit_bytes=_VMEM_LIMIT_BYTES)


def _rms_mod(x, g, shift, scale):
    ms = jnp.mean(x * x, axis=-1, keepdims=True)
    return (x * lax.rsqrt(ms + RMS_EPS)) * (g * (1.0 + scale)) + shift


def _silu(x):
    return x * jax.nn.sigmoid(x)


def _adaln_kernel(c_ref, w_ref, b_ref, o_ref):
    c = c_ref[...]
    s = _silu(c)
    o_ref[...] = jnp.dot(s, w_ref[...], preferred_element_type=F32,
                         precision=lax.Precision.HIGHEST) + b_ref[...]


def _adaln(c_pad, w, b, *, tn=1536):
    n_layers, d, d3 = w.shape
    bp = c_pad.shape[0]
    return pl.pallas_call(
        _adaln_kernel,
        grid=(n_layers, d3 // tn),
        in_specs=[
            pl.BlockSpec((bp, d), lambda l, j: (0, 0)),
            pl.BlockSpec((None, d, tn), lambda l, j: (l, 0, j)),
            pl.BlockSpec((None, 1, tn), lambda l, j: (l, 0, j)),
        ],
        out_specs=pl.BlockSpec((None, bp, tn), lambda l, j: (l, 0, j)),
        out_shape=jax.ShapeDtypeStruct((n_layers, bp, d3), F32),
        compiler_params=_params(2),
        name="adaln_mod",
    )(c_pad, w, b.reshape(n_layers, 1, d3))


_ROW_STRIDE = 2


def _stage_rows(slab_ref, t0, rows):
    slab_ref[pl.ds(_ROW_STRIDE * t0, rows.shape[0], stride=_ROW_STRIDE), :] = rows


def _stage_slab(ubuf_ref, slab, halo, u, *, pad, tm):
    _stage_rows(ubuf_ref.at[slab], 0, halo)
    _stage_rows(ubuf_ref.at[slab], pad, u)


def _dwconv_block(ubuf_ref, slab, r0, w_ref, lanes, *, pad, width, rows):
    acc = None
    slab_ref = ubuf_ref.at[slab]
    for k in range(width):
        start = _ROW_STRIDE * (pad + r0 - (width - 1) + k)
        term = w_ref[k:k + 1, lanes] * slab_ref[pl.ds(start, rows, stride=_ROW_STRIDE), :]
        acc = term if acc is None else acc + term
    return acc


def _pw1_kernel(x_ref, g_ref, shift_ref, scale_ref, wv_ref, wg_ref, bv_ref, bg_ref,
                dww_ref, dwb_ref, o_ref, h_ref, carry_ref, ubuf_ref, *, tm, tn_sub, pad, width):
    i = pl.program_id(1)
    j = pl.program_id(2)

    @pl.when(j == 0)
    def _():
        h_ref[...] = _rms_mod(x_ref[...], g_ref[...], shift_ref[...], scale_ref[...]).astype(BF16)

        @pl.when(i == 0)
        def _():
            carry_ref[...] = jnp.zeros(carry_ref.shape, F32)

    h = h_ref[...]
    rb = _CONV_ROW_BLOCK
    def project(c0):
        cols = slice(c0, c0 + tn_sub)
        val = jnp.dot(h, wv_ref[:, cols], preferred_element_type=F32) + bv_ref[:, cols]
        gt = jnp.dot(h, wg_ref[:, cols], preferred_element_type=F32) + bg_ref[:, cols]
        u = val * jax.nn.sigmoid(gt)
        halo = carry_ref[j, :, cols]
        carry_ref[j, :, cols] = u[tm - pad:tm, :]
        for s in range(tn_sub // _LANES):
            _stage_slab(ubuf_ref, c0 // _LANES + s, halo[:, s * _LANES:(s + 1) * _LANES],
                        u[:, s * _LANES:(s + 1) * _LANES], pad=pad, tm=tm)

    def convolve(c0):
        for s in range(tn_sub // _LANES):
            slab = c0 // _LANES + s
            lanes = slice(c0 + s * _LANES, c0 + (s + 1) * _LANES)
            for r0 in range(0, tm, rb):
                y = _dwconv_block(ubuf_ref, slab, r0, dww_ref, lanes, pad=pad, width=width, rows=rb)
                o_ref[r0:r0 + rb, lanes] = y + dwb_ref[:, lanes]

    starts = list(range(0, o_ref.shape[1], tn_sub))
    project(starts[0])
    for n, c0 in enumerate(starts):
        if n + 1 < len(starts):
            project(starts[n + 1])
        convolve(c0)


def _conformer_in(x, g, mod, w_bf, b, dw_w, dw_b, *, tm=512, tn=1024, tn_sub=256):
    bsz, t_len, d = x.shape
    width = dw_w.shape[0]
    pad = -(-(width - 1) // _SUBLANES) * _SUBLANES
    nj = d // tn
    kern = functools.partial(_pw1_kernel, tm=tm, tn_sub=tn_sub, pad=pad, width=width)
    return pl.pallas_call(
        kern,
        grid=(bsz, t_len // tm, nj),
        in_specs=[
            pl.BlockSpec((None, tm, d), lambda b_, i, j: (b_, i, 0)),
            pl.BlockSpec((1, d), lambda b_, i, j: (0, 0)),
            pl.BlockSpec((None, 1, d), lambda b_, i, j: (b_, 0, 0)),
            pl.BlockSpec((None, 1, d), lambda b_, i, j: (b_, 0, 1)),
            pl.BlockSpec((d, tn), lambda b_, i, j: (0, j)),
            pl.BlockSpec((d, tn), lambda b_, i, j: (0, nj + j)),
            pl.BlockSpec((1, tn), lambda b_, i, j: (0, j)),
            pl.BlockSpec((1, tn), lambda b_, i, j: (0, nj + j)),
            pl.BlockSpec((width, tn), lambda b_, i, j: (0, j)),
            pl.BlockSpec((1, tn), lambda b_, i, j: (0, j)),
        ],
        out_specs=pl.BlockSpec((None, tm, tn), lambda b_, i, j: (b_, i, j)),
        out_shape=jax.ShapeDtypeStruct((bsz, t_len, d), F32),
        scratch_shapes=[
            pltpu.VMEM((tm, d), BF16),
            pltpu.VMEM((nj, pad, tn), F32),
            pltpu.VMEM((tn // _LANES, _ROW_STRIDE * (pad + tm), _LANES), F32),
        ],
        compiler_params=_params(3),
        name="conformer_pw1_glu_conv",
    )(x, g.reshape(1, d), mod, mod, w_bf, w_bf, b.reshape(1, -1), b.reshape(1, -1),
      dw_w, dw_b.reshape(1, d))


_PROJ_COL_CHUNK = 512


def _proj_res_kernel(a_ref, lg_ref, lb_ref, w_ref, b_ref, x_ref, gate_ref, fg_ref, o_ref, *,
                     pre_ln, post_norm):
    a = a_ref[...]
    if pre_ln:
        mu = jnp.mean(a, axis=-1, keepdims=True)
        ac = a - mu
        var = jnp.mean(ac * ac, axis=-1, keepdims=True)
        a = _silu(ac * lax.rsqrt(var + LN_EPS) * lg_ref[...] + lb_ref[...]).astype(BF16)
    d = o_ref.shape[1]
    for c0 in range(0, d, _PROJ_COL_CHUNK):
        cols = slice(c0, c0 + _PROJ_COL_CHUNK)
        y = jnp.dot(a, w_ref[:, cols], preferred_element_type=F32) + b_ref[:, cols]
        o_ref[:, cols] = x_ref[:, cols] + gate_ref[:, cols] * y
    if post_norm:
        r = o_ref[...]
        ms = jnp.mean(r * r, axis=-1, keepdims=True)
        o_ref[...] = r * lax.rsqrt(ms + RMS_EPS) * fg_ref[...]


def _proj_residual(a, w_bf, b, x, mod, *, ln=None, final_g=None, tm=512):
    bsz, t_len, kdim = a.shape
    d = x.shape[-1]
    pre_ln, post_norm = ln is not None, final_g is not None
    lg, lb = ln if pre_ln else (jnp.zeros((kdim,), F32), jnp.zeros((kdim,), F32))
    fg = final_g if post_norm else jnp.zeros((d,), F32)
    kern = functools.partial(_proj_res_kernel, pre_ln=pre_ln, post_norm=post_norm)
    row_k = pl.BlockSpec((1, kdim), lambda b_, i: (0, 0))
    row_d = pl.BlockSpec((1, d), lambda b_, i: (0, 0))
    tile = pl.BlockSpec((None, tm, d), lambda b_, i: (b_, i, 0))
    return pl.pallas_call(
        kern,
        grid=(bsz, t_len // tm),
        in_specs=[
            pl.BlockSpec((None, tm, kdim), lambda b_, i: (b_, i, 0)),
            row_k, row_k,
            pl.BlockSpec((kdim, d), lambda b_, i: (0, 0), pipeline_mode=pl.Buffered(1)),
            row_d,
            tile,
            pl.BlockSpec((None, 1, d), lambda b_, i: (b_, 0, 2)),
            row_d,
        ],
        out_specs=tile,
        out_shape=jax.ShapeDtypeStruct(x.shape, F32),
        compiler_params=_params(2),
        name="ln_silu_proj_residual" if pre_ln else "proj_residual",
    )(a, lg.reshape(1, kdim), lb.reshape(1, kdim), w_bf, b.reshape(1, d), x, mod, fg.reshape(1, d))


def _ffn_up_kernel(x_ref, g_ref, shift_ref, scale_ref, wg_ref, wv_ref, dwg_ref, dwv_ref,
                   bg_ref, bv_ref, o_ref, h_ref, carry_ref, ubuf_ref, *, tm, tm_sub, tn_sub, pad, width):
    i = pl.program_id(1)
    j = pl.program_id(2)

    @pl.when(j == 0)
    def _():
        h_ref[...] = _rms_mod(x_ref[...], g_ref[...], shift_ref[...], scale_ref[...]).astype(BF16)

        @pl.when(i == 0)
        def _():
            carry_ref[...] = jnp.zeros(carry_ref.shape, F32)

    rb = _CONV_ROW_BLOCK
    sources = ((wg_ref, dwg_ref, bg_ref), (wv_ref, dwv_ref, bv_ref))
    col_starts = list(range(0, o_ref.shape[1], tn_sub))
    units = [(c0, 0, tm) for c0 in col_starts[:-1]]
    units += [(col_starts[-1], m0, tm_sub) for m0 in range(0, tm, tm_sub)]
    h = h_ref[...]
    for c0, m0, rows in units:
        cols = slice(c0, c0 + tn_sub)
        for half, (w_ref, _, _) in enumerate(sources):
            u = jnp.dot(h[m0:m0 + rows], w_ref[:, cols], preferred_element_type=F32)
            if m0 == 0:
                halo = carry_ref[half, j, :, cols]
            if m0 + rows == tm:
                carry_ref[half, j, :, cols] = u[rows - pad:, :]
            for s in range(tn_sub // _LANES):
                slab_ref = ubuf_ref.at[half].at[c0 // _LANES + s]
                if m0 == 0:
                    _stage_rows(slab_ref, 0, halo[:, s * _LANES:(s + 1) * _LANES])
                _stage_rows(slab_ref, pad + m0, u[:, s * _LANES:(s + 1) * _LANES])
        for s in range(tn_sub // _LANES):
            slab = c0 // _LANES + s
            lanes = slice(c0 + s * _LANES, c0 + (s + 1) * _LANES)
            for r0 in range(m0, m0 + rows, rb):
                gt, val = [
                    _dwconv_block(ubuf_ref.at[half], slab, r0, dw_ref, lanes, pad=pad, width=width, rows=rb)
                    + b_ref[:, lanes]
                    for half, (_, dw_ref, b_ref) in enumerate(sources)]
                o_ref[r0:r0 + rb, lanes] = (_silu(gt) * val).astype(o_ref.dtype)


def _ffn_up(x, g, mod, w_bf, dw_w, dw_b, *, tm=1024, tm_sub=1024, tn=512, tn_sub=256):
    bsz, t_len, d = x.shape
    f2 = w_bf.shape[1]
    f = f2 // 2
    width = dw_w.shape[0]
    pad = -(-(width - 1) // _SUBLANES) * _SUBLANES
    nj = f // tn
    kern = functools.partial(_ffn_up_kernel, tm=tm, tm_sub=tm_sub, tn_sub=tn_sub, pad=pad, width=width)
    return pl.pallas_call(
        kern,
        grid=(bsz, t_len // tm, nj),
        in_specs=[
            pl.BlockSpec((None, tm, d), lambda b_, i, j: (b_, i, 0)),
            pl.BlockSpec((1, d), lambda b_, i, j: (0, 0)),
            pl.BlockSpec((None, 1, d), lambda b_, i, j: (b_, 0, 0)),
            pl.BlockSpec((None, 1, d), lambda b_, i, j: (b_, 0, 1)),
            pl.BlockSpec((d, tn), lambda b_, i, j: (0, j)),
            pl.BlockSpec((d, tn), lambda b_, i, j: (0, nj + j)),
            pl.BlockSpec((width, tn), lambda b_, i, j: (0, j)),
            pl.BlockSpec((width, tn), lambda b_, i, j: (0, nj + j)),
            pl.BlockSpec((1, tn), lambda b_, i, j: (0, j)),
            pl.BlockSpec((1, tn), lambda b_, i, j: (0, nj + j)),
        ],
        out_specs=pl.BlockSpec((None, tm, tn), lambda b_, i, j: (b_, i, j)),
        out_shape=jax.ShapeDtypeStruct((bsz, t_len, f), BF16),
        scratch_shapes=[
            pltpu.VMEM((tm, d), BF16),
            pltpu.VMEM((2, nj, pad, tn), F32),
            pltpu.VMEM((2, tn // _LANES, _ROW_STRIDE * (pad + tm), _LANES), F32),
        ],
        compiler_params=_params(3),
        name="ffn_up_conv_swiglu",
    )(x, g.reshape(1, d), mod, mod, w_bf, w_bf, dw_w, dw_w, dw_b.reshape(1, f2), dw_b.reshape(1, f2))


def _norm_proj_kernel(x_ref, g_ref, shift_ref, scale_ref, w_ref, o_ref, *, q_cols, q_scale):
    h = _rms_mod(x_ref[...], g_ref[...], shift_ref[...], scale_ref[...]).astype(BF16)
    for c0 in range(0, o_ref.shape[1], _PROJ_COL_CHUNK):
        cols = slice(c0, c0 + _PROJ_COL_CHUNK)
        y = jnp.dot(h, w_ref[:, cols], preferred_element_type=F32)
        if c0 < q_cols:
            y = y * q_scale
        o_ref[:, cols] = y.astype(o_ref.dtype)


def _norm_proj(x, g, mod, w_bf, *, tm=512):
    bsz, t_len, d = x.shape
    n_out = w_bf.shape[1]
    kern = functools.partial(_norm_proj_kernel, q_cols=d, q_scale=1.0 / math.sqrt(d // N_HEADS))
    return pl.pallas_call(
        kern,
        grid=(bsz, t_len // tm),
        in_specs=[
            pl.BlockSpec((None, tm, d), lambda b_, i: (b_, i, 0)),
            pl.BlockSpec((1, d), lambda b_, i: (0, 0)),
            pl.BlockSpec((None, 1, d), lambda b_, i: (b_, 0, 0)),
            pl.BlockSpec((None, 1, d), lambda b_, i: (b_, 0, 1)),
            pl.BlockSpec((d, n_out), lambda b_, i: (0, 0), pipeline_mode=pl.Buffered(1)),
        ],
        out_specs=pl.BlockSpec((None, tm, n_out), lambda b_, i: (b_, i, 0)),
        out_shape=jax.ShapeDtypeStruct((bsz, t_len, n_out), BF16),
        compiler_params=_params(2),
        name="norm_qkv_proj",
    )(x, g.reshape(1, d), mod, mod, w_bf)


_HEADS_PER_STEP = 4
_KEY_CHUNK = 256
_SUFFIX_BLOCK = 128
_SIGN_BIT = 0x80000000
_UNDERFLOW_SUM = 120.0


def _neg_abs(x):
    bits = lax.bitcast_convert_type(x, jnp.uint32) | jnp.uint32(_SIGN_BIT)
    return lax.bitcast_convert_type(bits, F32)


def _sb_attn_kernel(q_ref, k_ref, v_ref, o_ref, acc_ref, carry_ref, *, tq, dh, n_grp):
    i = pl.program_id(2)
    tk, sb = _KEY_CHUNK, _SUFFIX_BLOCK
    n_diag = tq // tk
    row = lax.broadcasted_iota(jnp.int32, (2 * sb, 2 * sb), 0)
    col = lax.broadcasted_iota(jnp.int32, (2 * sb, 2 * sb), 1)
    suffix_op = (((row % sb) >= col) | (col >= sb)).astype(BF16)
    qpos = i * tq + lax.broadcasted_iota(jnp.int32, (tq, tk), 0)
    kcol = lax.broadcasted_iota(jnp.int32, (tq, tk), 1)

    carry_ref[...] = jnp.zeros(carry_ref.shape, F32)
    acc_ref[...] = jnp.zeros(acc_ref.shape, F32)

    def chunk(kb, masked, row0=0):
        start = pl.multiple_of(kb * tk, tk)
        if masked:
            causal = ((start + kcol) < qpos)[row0:]
        for g in range(n_grp):
            lanes = slice(g * dh, (g + 1) * dh)
            q = q_ref[row0:, lanes]
            k = k_ref[pl.ds(start, tk), lanes]
            v = v_ref[pl.ds(start, tk), lanes]
            z = lax.dot_general(q, k, (((1,), (1,)), ((), ())), preferred_element_type=F32)
            p = jnp.maximum(z, 0.0) + jnp.log(1.0 + jnp.exp(_neg_abs(z)))
            if masked:
                p = jnp.where(causal, p, 0.0)
            hi = p.astype(BF16)
            lo = (p - hi.astype(F32)).astype(BF16)
            tail = carry_ref[g, row0:]
            logits = [None] * (tk // sb)
            for blk in reversed(range(tk // sb)):
                cols = slice(blk * sb, (blk + 1) * sb)
                stacked = jnp.concatenate([hi[:, cols], lo[:, cols]], axis=1)
                sums = jnp.dot(stacked, suffix_op, preferred_element_type=F32)
                logits[blk] = (z[:, cols] - tail) - sums[:, :sb]
                tail = tail + sums[:, sb:]
            a = jnp.exp(jnp.concatenate(logits, axis=1))
            if masked:
                a = jnp.where(causal, a, 0.0)
            acc_ref[g, row0:] += jnp.dot(a.astype(BF16), v, preferred_element_type=F32)
            carry_ref[g, row0:] = tail

    for dg in range(n_diag):
        chunk(i * n_diag + (n_diag - 1 - dg), True, row0=(n_diag - 1 - dg) * tk)

    n_off = i * n_diag

    def keep_going(state):
        n, live = state
        return jnp.logical_and(n < n_off, live)

    def body(state):
        n, _ = state
        chunk(n_off - 1 - n, False)
        return n + 1, jnp.min(carry_ref[...]) < _UNDERFLOW_SUM

    lax.while_loop(keep_going, body, (jnp.int32(0), jnp.bool_(True)))
    for g in range(n_grp):
        o_ref[:, g * dh:(g + 1) * dh] = acc_ref[g].astype(o_ref.dtype)


def _sb_attention(qkv, *, tq=512):
    bsz, t_len, d3 = qkv.shape
    d = d3 // 3
    dh = d // N_HEADS
    n_grp = _HEADS_PER_STEP
    wblk = n_grp * dh
    nblk = d // wblk
    kern = functools.partial(_sb_attn_kernel, tq=tq, dh=dh, n_grp=n_grp)
    return pl.pallas_call(
        kern,
        grid=(bsz, nblk, t_len // tq),
        in_specs=[
            pl.BlockSpec((None, tq, wblk), lambda b_, h, i: (b_, i, h)),
            pl.BlockSpec((None, t_len, wblk), lambda b_, h, i: (b_, 0, nblk + h)),
            pl.BlockSpec((None, t_len, wblk), lambda b_, h, i: (b_, 0, 2 * nblk + h)),
        ],
        out_specs=pl.BlockSpec((None, tq, wblk), lambda b_, h, i: (b_, i, h)),
        out_shape=jax.ShapeDtypeStruct((bsz, t_len, d), BF16),
        scratch_shapes=[
            pltpu.VMEM((n_grp, tq, dh), F32),
            pltpu.VMEM((n_grp, tq, _SUFFIX_BLOCK), F32),
        ],
        compiler_params=_params(3),
        name="stick_breaking_attention",
    )(qkv, qkv, qkv)


def kernel(x, c, mix_norm_g, mix_mod_w, mix_mod_b, cv_pw1_w, cv_pw1_b, cv_dw_w, cv_dw_b, cv_ln_g, cv_ln_b, cv_pw2_w, cv_pw2_b, sb_qkv_w, sb_o_w, ffn_norm_g, ffn_mod_w, ffn_mod_b, ffn_up_w, ffn_dw_w, ffn_dw_b, ffn_down_w, final_norm_g):
    bsz, t_len, d = x.shape
    depth = mix_norm_g.shape[0]
    c_pad = jnp.pad(c, ((0, (-bsz) % _SUBLANES), (0, 0)))
    mix_mod = _adaln(c_pad, mix_mod_w, mix_mod_b)[:, :bsz].reshape(depth, bsz, 1, 3 * d)
    ffn_mod = _adaln(c_pad, ffn_mod_w, ffn_mod_b)[:, :bsz].reshape(depth, bsz, 1, 3 * d)
    zero_bias = jnp.zeros((d,), F32)

    for i in range(depth):
        j = i // 2
        if i % 2 == 0:
            conv = _conformer_in(x, mix_norm_g[i], mix_mod[i], cv_pw1_w[j].astype(BF16), cv_pw1_b[j],
                                 cv_dw_w[j], cv_dw_b[j])
            x = _proj_residual(conv, cv_pw2_w[j].astype(BF16), cv_pw2_b[j], x, mix_mod[i],
                               ln=(cv_ln_g[j], cv_ln_b[j]))
        else:
            qkv = _norm_proj(x, mix_norm_g[i], mix_mod[i], sb_qkv_w[j].astype(BF16))
            o = _sb_attention(qkv)
            x = _proj_residual(o, sb_o_w[j].astype(BF16), zero_bias, x, mix_mod[i])
        act = _ffn_up(x, ffn_norm_g[i], ffn_mod[i], ffn_up_w[i].astype(BF16), ffn_dw_w[i], ffn_dw_b[i])
        x = _proj_residual(act, ffn_down_w[i].astype(BF16), zero_bias, x, ffn_mod[i],
                           final_g=final_norm_g if i == depth - 1 else None)
    return x
```

```python
import functools
import math

import jax
import jax.numpy as jnp
from jax import lax
from jax.experimental import pallas as pl
from jax.experimental.pallas import tpu as pltpu

N_HEADS = 16
RMS_EPS = 1e-6
LN_EPS = 1e-5

F32 = jnp.float32
BF16 = jnp.bfloat16

_VMEM_LIMIT_BYTES = 56 * 1024 * 1024
_SUBLANES = 8
_LANES = 128
_CONV_ROW_BLOCK = 64


def _params(n_axes):
    return pltpu.CompilerParams(
        dimension_semantics=("arbitrary",) * n_axes,
        vmem_limit_bytes=_VMEM_LIMIT_BYTES)


def _rms_mod(x, g, shift, scale):
    ms = jnp.mean(x * x, axis=-1, keepdims=True)
    return (x * lax.rsqrt(ms + RMS_EPS)) * (g * (1.0 + scale)) + shift


def _silu(x):
    return x * jax.nn.sigmoid(x)


def _adaln_kernel(c_ref, w_ref, b_ref, o_ref):
    c = c_ref[...]
    s = _silu(c)
    o_ref[...] = jnp.dot(s, w_ref[...], preferred_element_type=F32,
                         precision=lax.Precision.HIGHEST) + b_ref[...]


def _adaln(c_pad, w, b, *, tn=1536):
    n_layers, d, d3 = w.shape
    bp = c_pad.shape[0]
    return pl.pallas_call(
        _adaln_kernel,
        grid=(n_layers, d3 // tn),
        in_specs=[
            pl.BlockSpec((bp, d), lambda l, j: (0, 0)),
            pl.BlockSpec((None, d, tn), lambda l, j: (l, 0, j)),
            pl.BlockSpec((None, 1, tn), lambda l, j: (l, 0, j)),
        ],
        out_specs=pl.BlockSpec((None, bp, tn), lambda l, j: (l, 0, j)),
        out_shape=jax.ShapeDtypeStruct((n_layers, bp, d3), F32),
        compiler_params=_params(2),
        name="adaln_mod",
    )(c_pad, w, b.reshape(n_layers, 1, d3))


_ROW_STRIDE = 2


def _stage_rows(slab_ref, t0, rows):
    slab_ref[pl.ds(_ROW_STRIDE * t0, rows.shape[0], stride=_ROW_STRIDE), :] = rows


def _stage_slab(ubuf_ref, slab, halo, u, *, pad, tm):
    _stage_rows(ubuf_ref.at[slab], 0, halo)
    _stage_rows(ubuf_ref.at[slab], pad, u)


def _dwconv_block(ubuf_ref, slab, r0, w_ref, lanes, *, pad, width, rows):
    acc = None
    slab_ref = ubuf_ref.at[slab]
    for k in range(width):
        start = _ROW_STRIDE * (pad + r0 - (width - 1) + k)
        term = w_ref[k:k + 1, lanes] * slab_ref[pl.ds(start, rows, stride=_ROW_STRIDE), :]
        acc = term if acc is None else acc + term
    return acc


def _pw1_kernel(x_ref, g_ref, shift_ref, scale_ref, wv_ref, wg_ref, bv_ref, bg_ref,
                dww_ref, dwb_ref, o_ref, h_ref, carry_ref, ubuf_ref, *, tm, tn_sub, pad, width):
    i = pl.program_id(1)
    j = pl.program_id(2)

    @pl.when(j == 0)
    def _():
        h_ref[...] = _rms_mod(x_ref[...], g_ref[...], shift_ref[...], scale_ref[...]).astype(BF16)

        @pl.when(i == 0)
        def _():
            carry_ref[...] = jnp.zeros(carry_ref.shape, F32)

    h = h_ref[...]
    rb = _CONV_ROW_BLOCK
    def project(c0):
        cols = slice(c0, c0 + tn_sub)
        val = jnp.dot(h, wv_ref[:, cols], preferred_element_type=F32) + bv_ref[:, cols]
        gt = jnp.dot(h, wg_ref[:, cols], preferred_element_type=F32) + bg_ref[:, cols]
        u = val * jax.nn.sigmoid(gt)
        halo = carry_ref[j, :, cols]
        carry_ref[j, :, cols] = u[tm - pad:tm, :]
        for s in range(tn_sub // _LANES):
            _stage_slab(ubuf_ref, c0 // _LANES + s, halo[:, s * _LANES:(s + 1) * _LANES],
                        u[:, s * _LANES:(s + 1) * _LANES], pad=pad, tm=tm)

    def convolve(c0):
        for s in range(tn_sub // _LANES):
            slab = c0 // _LANES + s
            lanes = slice(c0 + s * _LANES, c0 + (s + 1) * _LANES)
            for r0 in range(0, tm, rb):
                y = _dwconv_block(ubuf_ref, slab, r0, dww_ref, lanes, pad=pad, width=width, rows=rb)
                o_ref[r0:r0 + rb, lanes] = y + dwb_ref[:, lanes]

    starts = list(range(0, o_ref.shape[1], tn_sub))
    project(starts[0])
    for n, c0 in enumerate(starts):
        if n + 1 < len(starts):
            project(starts[n + 1])
        convolve(c0)


def _conformer_in(x, g, mod, w_bf, b, dw_w, dw_b, *, tm=512, tn=1024, tn_sub=256):
    bsz, t_len, d = x.shape
    width = dw_w.shape[0]
    pad = -(-(width - 1) // _SUBLANES) * _SUBLANES
    nj = d // tn
    kern = functools.partial(_pw1_kernel, tm=tm, tn_sub=tn_sub, pad=pad, width=width)
    return pl.pallas_call(
        kern,
        grid=(bsz, t_len // tm, nj),
        in_specs=[
            pl.BlockSpec((None, tm, d), lambda b_, i, j: (b_, i, 0)),
            pl.BlockSpec((1, d), lambda b_, i, j: (0, 0)),
            pl.BlockSpec((None, 1, d), lambda b_, i, j: (b_, 0, 0)),
            pl.BlockSpec((None, 1, d), lambda b_, i, j: (b_, 0, 1)),
            pl.BlockSpec((d, tn), lambda b_, i, j: (0, j)),
            pl.BlockSpec((d, tn), lambda b_, i, j: (0, nj + j)),
            pl.BlockSpec((1, tn), lambda b_, i, j: (0, j)),
            pl.BlockSpec((1, tn), lambda b_, i, j: (0, nj + j)),
            pl.BlockSpec((width, tn), lambda b_, i, j: (0, j)),
            pl.BlockSpec((1, tn), lambda b_, i, j: (0, j)),
        ],
        out_specs=pl.BlockSpec((None, tm, tn), lambda b_, i, j: (b_, i, j)),
        out_shape=jax.ShapeDtypeStruct((bsz, t_len, d), F32),
        scratch_shapes=[
            pltpu.VMEM((tm, d), BF16),
            pltpu.VMEM((nj, pad, tn), F32),
            pltpu.VMEM((tn // _LANES, _ROW_STRIDE * (pad + tm), _LANES), F32),
        ],
        compiler_params=_params(3),
        name="conformer_pw1_glu_conv",
    )(x, g.reshape(1, d), mod, mod, w_bf, w_bf, b.reshape(1, -1), b.reshape(1, -1),
      dw_w, dw_b.reshape(1, d))


_PROJ_COL_CHUNK = 512


def _proj_res_kernel(a_ref, lg_ref, lb_ref, w_ref, b_ref, x_ref, gate_ref, fg_ref, o_ref, *,
                     pre_ln, post_norm):
    a = a_ref[...]
    if pre_ln:
        mu = jnp.mean(a, axis=-1, keepdims=True)
        ac = a - mu
        var = jnp.mean(ac * ac, axis=-1, keepdims=True)
        a = _silu(ac * lax.rsqrt(var + LN_EPS) * lg_ref[...] + lb_ref[...]).astype(BF16)
    d = o_ref.shape[1]
    for c0 in range(0, d, _PROJ_COL_CHUNK):
        cols = slice(c0, c0 + _PROJ_COL_CHUNK)
        y = jnp.dot(a, w_ref[:, cols], preferred_element_type=F32) + b_ref[:, cols]
        o_ref[:, cols] = x_ref[:, cols] + gate_ref[:, cols] * y
    if post_norm:
        r = o_ref[...]
        ms = jnp.mean(r * r, axis=-1, keepdims=True)
        o_ref[...] = r * lax.rsqrt(ms + RMS_EPS) * fg_ref[...]


def _proj_residual(a, w_all, layer, b, x, mod, *, ln=None, final_g=None, tm=512):
    bsz, t_len, kdim = a.shape
    d = x.shape[-1]
    pre_ln, post_norm = ln is not None, final_g is not None
    lg, lb = ln if pre_ln else (jnp.zeros((kdim,), F32), jnp.zeros((kdim,), F32))
    fg = final_g if post_norm else jnp.zeros((d,), F32)
    kern = functools.partial(_proj_res_kernel, pre_ln=pre_ln, post_norm=post_norm)
    row_k = pl.BlockSpec((1, kdim), lambda b_, i: (0, 0))
    row_d = pl.BlockSpec((1, d), lambda b_, i: (0, 0))
    tile = pl.BlockSpec((None, tm, d), lambda b_, i: (b_, i, 0))
    return pl.pallas_call(
        kern,
        grid=(bsz, t_len // tm),
        in_specs=[
            pl.BlockSpec((None, tm, kdim), lambda b_, i: (b_, i, 0)),
            row_k, row_k,
            pl.BlockSpec((None, kdim, d), lambda b_, i: (layer, 0, 0), pipeline_mode=pl.Buffered(1)),
            row_d,
            tile,
            pl.BlockSpec((None, 1, d), lambda b_, i: (b_, 0, 2)),
            row_d,
        ],
        out_specs=tile,
        out_shape=jax.ShapeDtypeStruct(x.shape, F32),
        compiler_params=_params(2),
        name="ln_silu_proj_residual" if pre_ln else "proj_residual",
    )(a, lg.reshape(1, kdim), lb.reshape(1, kdim), w_all, b.reshape(1, d), x, mod, fg.reshape(1, d))


def _ffn_up_kernel(x_ref, g_ref, shift_ref, scale_ref, wg_ref, wv_ref, dwg_ref, dwv_ref,
                   bg_ref, bv_ref, o_ref, h_ref, carry_ref, ubuf_ref, *, tm, tm_sub, tn_sub, pad, width):
    i = pl.program_id(1)
    j = pl.program_id(2)

    @pl.when(j == 0)
    def _():
        h_ref[...] = _rms_mod(x_ref[...], g_ref[...], shift_ref[...], scale_ref[...]).astype(BF16)

        @pl.when(i == 0)
        def _():
            carry_ref[...] = jnp.zeros(carry_ref.shape, F32)

    rb = _CONV_ROW_BLOCK
    sources = ((wg_ref, dwg_ref, bg_ref), (wv_ref, dwv_ref, bv_ref))
    col_starts = list(range(0, o_ref.shape[1], tn_sub))
    units = [(c0, 0, tm) for c0 in col_starts[:-1]]
    units += [(col_starts[-1], m0, tm_sub) for m0 in range(0, tm, tm_sub)]
    h = h_ref[...]
    for c0, m0, rows in units:
        cols = slice(c0, c0 + tn_sub)
        for half, (w_ref, _, _) in enumerate(sources):
            u = jnp.dot(h[m0:m0 + rows], w_ref[:, cols].astype(BF16), preferred_element_type=F32)
            if m0 == 0:
                halo = carry_ref[half, j, :, cols]
            if m0 + rows == tm:
                carry_ref[half, j, :, cols] = u[rows - pad:, :]
            for s in range(tn_sub // _LANES):
                slab_ref = ubuf_ref.at[half].at[c0 // _LANES + s]
                if m0 == 0:
                    _stage_rows(slab_ref, 0, halo[:, s * _LANES:(s + 1) * _LANES])
                _stage_rows(slab_ref, pad + m0, u[:, s * _LANES:(s + 1) * _LANES])
        for s in range(tn_sub // _LANES):
            slab = c0 // _LANES + s
            lanes = slice(c0 + s * _LANES, c0 + (s + 1) * _LANES)
            for r0 in range(m0, m0 + rows, rb):
                gt, val = [
                    _dwconv_block(ubuf_ref.at[half], slab, r0, dw_ref, lanes, pad=pad, width=width, rows=rb)
                    + b_ref[:, lanes]
                    for half, (_, dw_ref, b_ref) in enumerate(sources)]
                o_ref[r0:r0 + rb, lanes] = (_silu(gt) * val).astype(o_ref.dtype)


def _ffn_up(x, g, mod, w_all, layer, dw_w, dw_b, *, tm=1024, tm_sub=1024, tn=512, tn_sub=256):
    bsz, t_len, d = x.shape
    f2 = w_all.shape[2]
    f = f2 // 2
    width = dw_w.shape[0]
    pad = -(-(width - 1) // _SUBLANES) * _SUBLANES
    nj = f // tn
    kern = functools.partial(_ffn_up_kernel, tm=tm, tm_sub=tm_sub, tn_sub=tn_sub, pad=pad, width=width)
    return pl.pallas_call(
        kern,
        grid=(bsz, t_len // tm, nj),
        in_specs=[
            pl.BlockSpec((None, tm, d), lambda b_, i, j: (b_, i, 0)),
            pl.BlockSpec((1, d), lambda b_, i, j: (0, 0)),
            pl.BlockSpec((None, 1, d), lambda b_, i, j: (b_, 0, 0)),
            pl.BlockSpec((None, 1, d), lambda b_, i, j: (b_, 0, 1)),
            pl.BlockSpec((None, d, tn), lambda b_, i, j: (layer, 0, j)),
            pl.BlockSpec((None, d, tn), lambda b_, i, j: (layer, 0, nj + j)),
            pl.BlockSpec((width, tn), lambda b_, i, j: (0, j)),
            pl.BlockSpec((width, tn), lambda b_, i, j: (0, nj + j)),
            pl.BlockSpec((1, tn), lambda b_, i, j: (0, j)),
            pl.BlockSpec((1, tn), lambda b_, i, j: (0, nj + j)),
        ],
        out_specs=pl.BlockSpec((None, tm, tn), lambda b_, i, j: (b_, i, j)),
        out_shape=jax.ShapeDtypeStruct((bsz, t_len, f), BF16),
        scratch_shapes=[
            pltpu.VMEM((tm, d), BF16),
            pltpu.VMEM((2, nj, pad, tn), F32),
            pltpu.VMEM((2, tn // _LANES, _ROW_STRIDE * (pad + tm), _LANES), F32),
        ],
        compiler_params=_params(3),
        name="ffn_up_conv_swiglu",
    )(x, g.reshape(1, d), mod, mod, w_all, w_all, dw_w, dw_w, dw_b.reshape(1, f2), dw_b.reshape(1, f2))


def _norm_proj_kernel(x_ref, g_ref, shift_ref, scale_ref, w_ref, o_ref, *, q_cols, q_scale):
    h = _rms_mod(x_ref[...], g_ref[...], shift_ref[...], scale_ref[...]).astype(BF16)
    for c0 in range(0, o_ref.shape[1], _PROJ_COL_CHUNK):
        cols = slice(c0, c0 + _PROJ_COL_CHUNK)
        y = jnp.dot(h, w_ref[:, cols], preferred_element_type=F32)
        if c0 < q_cols:
            y = y * q_scale
        o_ref[:, cols] = y.astype(o_ref.dtype)


def _norm_proj(x, g, mod, w_bf, *, tm=512):
    bsz, t_len, d = x.shape
    n_out = w_bf.shape[1]
    kern = functools.partial(_norm_proj_kernel, q_cols=d, q_scale=1.0 / math.sqrt(d // N_HEADS))
    return pl.pallas_call(
        kern,
        grid=(bsz, t_len // tm),
        in_specs=[
            pl.BlockSpec((None, tm, d), lambda b_, i: (b_, i, 0)),
            pl.BlockSpec((1, d), lambda b_, i: (0, 0)),
            pl.BlockSpec((None, 1, d), lambda b_, i: (b_, 0, 0)),
            pl.BlockSpec((None, 1, d), lambda b_, i: (b_, 0, 1)),
            pl.BlockSpec((d, n_out), lambda b_, i: (0, 0), pipeline_mode=pl.Buffered(1)),
        ],
        out_specs=pl.BlockSpec((None, tm, n_out), lambda b_, i: (b_, i, 0)),
        out_shape=jax.ShapeDtypeStruct((bsz, t_len, n_out), BF16),
        compiler_params=_params(2),
        name="norm_qkv_proj",
    )(x, g.reshape(1, d), mod, mod, w_bf)


_HEADS_PER_STEP = 4
_KEY_CHUNK = 256
_SUFFIX_BLOCK = 128
_SIGN_BIT = 0x80000000
_UNDERFLOW_SUM = 120.0


def _neg_abs(x):
    bits = lax.bitcast_convert_type(x, jnp.uint32) | jnp.uint32(_SIGN_BIT)
    return lax.bitcast_convert_type(bits, F32)


def _sb_attn_kernel(q_ref, k_ref, v_ref, o_ref, acc_ref, carry_ref, *, tq, dh, n_grp):
    i = pl.program_id(2)
    tk, sb = _KEY_CHUNK, _SUFFIX_BLOCK
    n_diag = tq // tk
    row = lax.broadcasted_iota(jnp.int32, (2 * sb, 2 * sb), 0)
    col = lax.broadcasted_iota(jnp.int32, (2 * sb, 2 * sb), 1)
    suffix_op = (((row % sb) >= col) | (col >= sb)).astype(BF16)
    qpos = i * tq + lax.broadcasted_iota(jnp.int32, (tq, tk), 0)
    kcol = lax.broadcasted_iota(jnp.int32, (tq, tk), 1)

    carry_ref[...] = jnp.zeros(carry_ref.shape, F32)
    acc_ref[...] = jnp.zeros(acc_ref.shape, F32)

    def chunk(kb, masked, row0=0):
        start = pl.multiple_of(kb * tk, tk)
        if masked:
            causal = ((start + kcol) < qpos)[row0:]
        for g in range(n_grp):
            lanes = slice(g * dh, (g + 1) * dh)
            q = q_ref[row0:, lanes]
            k = k_ref[pl.ds(start, tk), lanes]
            v = v_ref[pl.ds(start, tk), lanes]
            z = lax.dot_general(q, k, (((1,), (1,)), ((), ())), preferred_element_type=F32)
            p = jnp.maximum(z, 0.0) + jnp.log(1.0 + jnp.exp(_neg_abs(z)))
            if masked:
                p = jnp.where(causal, p, 0.0)
            hi = p.astype(BF16)
            lo = (p - hi.astype(F32)).astype(BF16)
            tail = carry_ref[g, row0:]
            logits = [None] * (tk // sb)
            for blk in reversed(range(tk // sb)):
                cols = slice(blk * sb, (blk + 1) * sb)
                stacked = jnp.concatenate([hi[:, cols], lo[:, cols]], axis=1)
                sums = jnp.dot(stacked, suffix_op, preferred_element_type=F32)
                logits[blk] = (z[:, cols] - tail) - sums[:, :sb]
                tail = tail + sums[:, sb:]
            a = jnp.exp(jnp.concatenate(logits, axis=1))
            if masked:
                a = jnp.where(causal, a, 0.0)
            acc_ref[g, row0:] += jnp.dot(a.astype(BF16), v, preferred_element_type=F32)
            carry_ref[g, row0:] = tail

    for dg in range(n_diag):
        chunk(i * n_diag + (n_diag - 1 - dg), True, row0=(n_diag - 1 - dg) * tk)

    n_off = i * n_diag

    def keep_going(state):
        n, live = state
        return jnp.logical_and(n < n_off, live)

    def body(state):
        n, _ = state
        chunk(n_off - 1 - n, False)
        return n + 1, jnp.min(carry_ref[...]) < _UNDERFLOW_SUM

    lax.while_loop(keep_going, body, (jnp.int32(0), jnp.bool_(True)))
    for g in range(n_grp):
        o_ref[:, g * dh:(g + 1) * dh] = acc_ref[g].astype(o_ref.dtype)


def _sb_attention(qkv, *, tq=512):
    bsz, t_len, d3 = qkv.shape
    d = d3 // 3
    dh = d // N_HEADS
    n_grp = _HEADS_PER_STEP
    wblk = n_grp * dh
    nblk = d // wblk
    kern = functools.partial(_sb_attn_kernel, tq=tq, dh=dh, n_grp=n_grp)
    return pl.pallas_call(
        kern,
        grid=(bsz, nblk, t_len // tq),
        in_specs=[
            pl.BlockSpec((None, tq, wblk), lambda b_, h, i: (b_, i, h)),
            pl.BlockSpec((None, t_len, wblk), lambda b_, h, i: (b_, 0, nblk + h)),
            pl.BlockSpec((None, t_len, wblk), lambda b_, h, i: (b_, 0, 2 * nblk + h)),
        ],
        out_specs=pl.BlockSpec((None, tq, wblk), lambda b_, h, i: (b_, i, h)),
        out_shape=jax.ShapeDtypeStruct((bsz, t_len, d), BF16),
        scratch_shapes=[
            pltpu.VMEM((n_grp, tq, dh), F32),
            pltpu.VMEM((n_grp, tq, _SUFFIX_BLOCK), F32),
        ],
        compiler_params=_params(3),
        name="stick_breaking_attention",
    )(qkv, qkv, qkv)


def kernel(x, c, mix_norm_g, mix_mod_w, mix_mod_b, cv_pw1_w, cv_pw1_b, cv_dw_w, cv_dw_b, cv_ln_g, cv_ln_b, cv_pw2_w, cv_pw2_b, sb_qkv_w, sb_o_w, ffn_norm_g, ffn_mod_w, ffn_mod_b, ffn_up_w, ffn_dw_w, ffn_dw_b, ffn_down_w, final_norm_g):
    bsz, t_len, d = x.shape
    depth = mix_norm_g.shape[0]
    c_pad = jnp.pad(c, ((0, (-bsz) % _SUBLANES), (0, 0)))
    mix_mod = _adaln(c_pad, mix_mod_w, mix_mod_b)[:, :bsz].reshape(depth, bsz, 1, 3 * d)
    ffn_mod = _adaln(c_pad, ffn_mod_w, ffn_mod_b)[:, :bsz].reshape(depth, bsz, 1, 3 * d)
    zero_bias = jnp.zeros((d,), F32)
    pw2_w, o_w, down_w = cv_pw2_w.astype(BF16), sb_o_w.astype(BF16), ffn_down_w.astype(BF16)

    for i in range(depth):
        j = i // 2
        if i % 2 == 0:
            conv = _conformer_in(x, mix_norm_g[i], mix_mod[i], cv_pw1_w[j].astype(BF16), cv_pw1_b[j],
                                 cv_dw_w[j], cv_dw_b[j])
            x = _proj_residual(conv, pw2_w, j, cv_pw2_b[j], x, mix_mod[i], ln=(cv_ln_g[j], cv_ln_b[j]))
        else:
            qkv = _norm_proj(x, mix_norm_g[i], mix_mod[i], sb_qkv_w[j].astype(BF16))
            o = _sb_attention(qkv)
            x = _proj_residual(o, o_w, j, zero_bias, x, mix_mod[i])
        act = _ffn_up(x, ffn_norm_g[i], ffn_mod[i], ffn_up_w, i, ffn_dw_w[i], ffn_dw_b[i])
        x = _proj_residual(act, down_w, i, zero_bias, x, ffn_mod[i],
                           final_g=final_norm_g if i == depth - 1 else None)
    return x
```

```python
import functools
import math

import jax
import jax.numpy as jnp
from jax import lax
from jax.experimental import pallas as pl
from jax.experimental.pallas import tpu as pltpu

N_HEADS = 16
RMS_EPS = 1e-6
LN_EPS = 1e-5

F32 = jnp.float32
BF16 = jnp.bfloat16

_VMEM_LIMIT_BYTES = 56 * 1024 * 1024
_SUBLANES = 8
_LANES = 128
_MXU_COLS = 256
_CONV_ROW_BLOCK = 64

_TM_RESIDENT = 512
_TM_CONFORMER, _TN_CONFORMER = 512, 1024
_TM_FFN_UP, _TN_FFN_UP = 1024, 512
_TN_ADALN = 1536
_TQ_ATTENTION = 512


def _params(n_axes):
    return pltpu.CompilerParams(
        dimension_semantics=("arbitrary",) * n_axes,
        vmem_limit_bytes=_VMEM_LIMIT_BYTES)


def _rms_mod(x, g, shift, scale):
    ms = jnp.mean(x * x, axis=-1, keepdims=True)
    return (x * lax.rsqrt(ms + RMS_EPS)) * (g * (1.0 + scale)) + shift


def _silu(x):
    return x * jax.nn.sigmoid(x)


def _adaln_kernel(c_ref, w_ref, b_ref, o_ref):
    c = c_ref[...]
    s = _silu(c)
    o_ref[...] = jnp.dot(s, w_ref[...], preferred_element_type=F32,
                         precision=lax.Precision.HIGHEST) + b_ref[...]


def _adaln(c_pad, w, b, *, tn=_TN_ADALN):
    n_layers, d, d3 = w.shape
    bp = c_pad.shape[0]
    return pl.pallas_call(
        _adaln_kernel,
        grid=(n_layers, d3 // tn),
        in_specs=[
            pl.BlockSpec((bp, d), lambda l, j: (0, 0)),
            pl.BlockSpec((None, d, tn), lambda l, j: (l, 0, j)),
            pl.BlockSpec((None, 1, tn), lambda l, j: (l, 0, j)),
        ],
        out_specs=pl.BlockSpec((None, bp, tn), lambda l, j: (l, 0, j)),
        out_shape=jax.ShapeDtypeStruct((n_layers, bp, d3), F32),
        compiler_params=_params(2),
        name="adaln_mod",
    )(c_pad, w, b.reshape(n_layers, 1, d3))


_ROW_STRIDE = 2


def _stage_rows(slab_ref, t0, rows):
    slab_ref[pl.ds(_ROW_STRIDE * t0, rows.shape[0], stride=_ROW_STRIDE), :] = rows


def _stage_slab(ubuf_ref, slab, halo, u, *, pad, tm):
    _stage_rows(ubuf_ref.at[slab], 0, halo)
    _stage_rows(ubuf_ref.at[slab], pad, u)


def _dwconv_block(ubuf_ref, slab, r0, w_ref, lanes, *, pad, width, rows):
    acc = None
    slab_ref = ubuf_ref.at[slab]
    for k in range(width):
        start = _ROW_STRIDE * (pad + r0 - (width - 1) + k)
        term = w_ref[k:k + 1, lanes] * slab_ref[pl.ds(start, rows, stride=_ROW_STRIDE), :]
        acc = term if acc is None else acc + term
    return acc


def _pw1_kernel(x_ref, g_ref, shift_ref, scale_ref, wv_ref, wg_ref, bv_ref, bg_ref,
                dww_ref, dwb_ref, o_ref, h_ref, carry_ref, ubuf_ref, *, tm, tn_sub, pad, width):
    i = pl.program_id(1)
    j = pl.program_id(2)

    @pl.when(j == 0)
    def _():
        h_ref[...] = _rms_mod(x_ref[...], g_ref[...], shift_ref[...], scale_ref[...]).astype(BF16)

        @pl.when(i == 0)
        def _():
            carry_ref[...] = jnp.zeros(carry_ref.shape, F32)

    h = h_ref[...]
    rb = _CONV_ROW_BLOCK
    for c0 in range(0, o_ref.shape[1], tn_sub):
        cols = slice(c0, c0 + tn_sub)
        val = jnp.dot(h, wv_ref[:, cols], preferred_element_type=F32) + bv_ref[:, cols]
        gt = jnp.dot(h, wg_ref[:, cols], preferred_element_type=F32) + bg_ref[:, cols]
        u = val * jax.nn.sigmoid(gt)
        halo = carry_ref[j, :, cols]
        carry_ref[j, :, cols] = u[tm - pad:tm, :]
        for s in range(tn_sub // _LANES):
            slab = c0 // _LANES + s
            lanes = slice(c0 + s * _LANES, c0 + (s + 1) * _LANES)
            _stage_slab(ubuf_ref, slab, halo[:, s * _LANES:(s + 1) * _LANES],
                        u[:, s * _LANES:(s + 1) * _LANES], pad=pad, tm=tm)
            for r0 in range(0, tm, rb):
                y = _dwconv_block(ubuf_ref, slab, r0, dww_ref, lanes, pad=pad, width=width, rows=rb)
                o_ref[r0:r0 + rb, lanes] = y + dwb_ref[:, lanes]


def _conformer_in(x, g, mod, w_bf, b, dw_w, dw_b, *, tm=_TM_CONFORMER, tn=_TN_CONFORMER,
                  tn_sub=_MXU_COLS):
    bsz, t_len, d = x.shape
    width = dw_w.shape[0]
    pad = -(-(width - 1) // _SUBLANES) * _SUBLANES
    nj = d // tn
    kern = functools.partial(_pw1_kernel, tm=tm, tn_sub=tn_sub, pad=pad, width=width)
    return pl.pallas_call(
        kern,
        grid=(bsz, t_len // tm, nj),
        in_specs=[
            pl.BlockSpec((None, tm, d), lambda b_, i, j: (b_, i, 0)),
            pl.BlockSpec((1, d), lambda b_, i, j: (0, 0)),
            pl.BlockSpec((None, 1, d), lambda b_, i, j: (b_, 0, 0)),
            pl.BlockSpec((None, 1, d), lambda b_, i, j: (b_, 0, 1)),
            pl.BlockSpec((d, tn), lambda b_, i, j: (0, j)),
            pl.BlockSpec((d, tn), lambda b_, i, j: (0, nj + j)),
            pl.BlockSpec((1, tn), lambda b_, i, j: (0, j)),
            pl.BlockSpec((1, tn), lambda b_, i, j: (0, nj + j)),
            pl.BlockSpec((width, tn), lambda b_, i, j: (0, j)),
            pl.BlockSpec((1, tn), lambda b_, i, j: (0, j)),
        ],
        out_specs=pl.BlockSpec((None, tm, tn), lambda b_, i, j: (b_, i, j)),
        out_shape=jax.ShapeDtypeStruct((bsz, t_len, d), F32),
        scratch_shapes=[
            pltpu.VMEM((tm, d), BF16),
            pltpu.VMEM((nj, pad, tn), F32),
            pltpu.VMEM((tn // _LANES, _ROW_STRIDE * (pad + tm), _LANES), F32),
        ],
        compiler_params=_params(3),
        name="conformer_pw1_glu_conv",
    )(x, g.reshape(1, d), mod, mod, w_bf, w_bf, b.reshape(1, -1), b.reshape(1, -1),
      dw_w, dw_b.reshape(1, d))


_PROJ_COL_CHUNK = 512


def _proj_res_kernel(a_ref, lg_ref, lb_ref, w_ref, b_ref, x_ref, gate_ref, fg_ref, o_ref, *,
                     pre_ln, post_norm):
    a = a_ref[...]
    if pre_ln:
        mu = jnp.mean(a, axis=-1, keepdims=True)
        ac = a - mu
        var = jnp.mean(ac * ac, axis=-1, keepdims=True)
        a = _silu(ac * lax.rsqrt(var + LN_EPS) * lg_ref[...] + lb_ref[...]).astype(BF16)
    d = o_ref.shape[1]
    for c0 in range(0, d, _PROJ_COL_CHUNK):
        cols = slice(c0, c0 + _PROJ_COL_CHUNK)
        y = jnp.dot(a, w_ref[:, cols], preferred_element_type=F32) + b_ref[:, cols]
        o_ref[:, cols] = x_ref[:, cols] + gate_ref[:, cols] * y
    if post_norm:
        r = o_ref[...]
        ms = jnp.mean(r * r, axis=-1, keepdims=True)
        o_ref[...] = r * lax.rsqrt(ms + RMS_EPS) * fg_ref[...]


def _proj_residual(a, w_all, layer, b, x, mod, *, ln=None, final_g=None, tm=_TM_RESIDENT):
    bsz, t_len, kdim = a.shape
    d = x.shape[-1]
    pre_ln, post_norm = ln is not None, final_g is not None
    lg, lb = ln if pre_ln else (jnp.zeros((kdim,), F32), jnp.zeros((kdim,), F32))
    fg = final_g if post_norm else jnp.zeros((d,), F32)
    kern = functools.partial(_proj_res_kernel, pre_ln=pre_ln, post_norm=post_norm)
    row_k = pl.BlockSpec((1, kdim), lambda b_, i: (0, 0))
    row_d = pl.BlockSpec((1, d), lambda b_, i: (0, 0))
    tile = pl.BlockSpec((None, tm, d), lambda b_, i: (b_, i, 0))
    return pl.pallas_call(
        kern,
        grid=(bsz, t_len // tm),
        in_specs=[
            pl.BlockSpec((None, tm, kdim), lambda b_, i: (b_, i, 0)),
            row_k, row_k,
            pl.BlockSpec((None, kdim, d), lambda b_, i: (layer, 0, 0), pipeline_mode=pl.Buffered(1)),
            row_d,
            tile,
            pl.BlockSpec((None, 1, d), lambda b_, i: (b_, 0, 2)),
            row_d,
        ],
        out_specs=tile,
        out_shape=jax.ShapeDtypeStruct(x.shape, F32),
        compiler_params=_params(2),
        name="ln_silu_proj_residual" if pre_ln else "proj_residual",
    )(a, lg.reshape(1, kdim), lb.reshape(1, kdim), w_all, b.reshape(1, d), x, mod, fg.reshape(1, d))


def _ffn_up_kernel(x_ref, g_ref, shift_ref, scale_ref, wg_ref, wv_ref, dwg_ref, dwv_ref,
                   bg_ref, bv_ref, o_ref, h_ref, carry_ref, ubuf_ref, *, tm, tn_sub, pad, width):
    i = pl.program_id(1)
    j = pl.program_id(2)

    @pl.when(j == 0)
    def _():
        h_ref[...] = _rms_mod(x_ref[...], g_ref[...], shift_ref[...], scale_ref[...]).astype(BF16)

        @pl.when(i == 0)
        def _():
            carry_ref[...] = jnp.zeros(carry_ref.shape, F32)

    rb = _CONV_ROW_BLOCK
    sources = ((wg_ref, dwg_ref, bg_ref), (wv_ref, dwv_ref, bv_ref))
    h = h_ref[...]
    for c0 in range(0, o_ref.shape[1], tn_sub):
        cols = slice(c0, c0 + tn_sub)
        for half, (w_ref, _, _) in enumerate(sources):
            u = jnp.dot(h, w_ref[:, cols].astype(BF16), preferred_element_type=F32)
            halo = carry_ref[half, j, :, cols]
            carry_ref[half, j, :, cols] = u[tm - pad:, :]
            for s in range(tn_sub // _LANES):
                _stage_slab(ubuf_ref.at[half], c0 // _LANES + s, halo[:, s * _LANES:(s + 1) * _LANES],
                            u[:, s * _LANES:(s + 1) * _LANES], pad=pad, tm=tm)
        for s in range(tn_sub // _LANES):
            slab = c0 // _LANES + s
            lanes = slice(c0 + s * _LANES, c0 + (s + 1) * _LANES)
            for r0 in range(0, tm, rb):
                gt, val = [
                    _dwconv_block(ubuf_ref.at[half], slab, r0, dw_ref, lanes, pad=pad, width=width, rows=rb)
                    + b_ref[:, lanes]
                    for half, (_, dw_ref, b_ref) in enumerate(sources)]
                o_ref[r0:r0 + rb, lanes] = (_silu(gt) * val).astype(o_ref.dtype)


def _ffn_up(x, g, mod, w_all, layer, dw_w, dw_b, *, tm=_TM_FFN_UP, tn=_TN_FFN_UP, tn_sub=_MXU_COLS):
    bsz, t_len, d = x.shape
    f2 = w_all.shape[2]
    f = f2 // 2
    width = dw_w.shape[0]
    pad = -(-(width - 1) // _SUBLANES) * _SUBLANES
    nj = f // tn
    kern = functools.partial(_ffn_up_kernel, tm=tm, tn_sub=tn_sub, pad=pad, width=width)
    return pl.pallas_call(
        kern,
        grid=(bsz, t_len // tm, nj),
        in_specs=[
            pl.BlockSpec((None, tm, d), lambda b_, i, j: (b_, i, 0)),
            pl.BlockSpec((1, d), lambda b_, i, j: (0, 0)),
            pl.BlockSpec((None, 1, d), lambda b_, i, j: (b_, 0, 0)),
            pl.BlockSpec((None, 1, d), lambda b_, i, j: (b_, 0, 1)),
            pl.BlockSpec((None, d, tn), lambda b_, i, j: (layer, 0, j)),
            pl.BlockSpec((None, d, tn), lambda b_, i, j: (layer, 0, nj + j)),
            pl.BlockSpec((width, tn), lambda b_, i, j: (0, j)),
            pl.BlockSpec((width, tn), lambda b_, i, j: (0, nj + j)),
            pl.BlockSpec((1, tn), lambda b_, i, j: (0, j)),
            pl.BlockSpec((1, tn), lambda b_, i, j: (0, nj + j)),
        ],
        out_specs=pl.BlockSpec((None, tm, tn), lambda b_, i, j: (b_, i, j)),
        out_shape=jax.ShapeDtypeStruct((bsz, t_len, f), BF16),
        scratch_shapes=[
            pltpu.VMEM((tm, d), BF16),
            pltpu.VMEM((2, nj, pad, tn), F32),
            pltpu.VMEM((2, tn // _LANES, _ROW_STRIDE * (pad + tm), _LANES), F32),
        ],
        compiler_params=_params(3),
        name="ffn_up_conv_swiglu",
    )(x, g.reshape(1, d), mod, mod, w_all, w_all, dw_w, dw_w, dw_b.reshape(1, f2), dw_b.reshape(1, f2))


def _norm_proj_kernel(x_ref, g_ref, shift_ref, scale_ref, w_ref, o_ref, *, q_cols, q_scale):
    h = _rms_mod(x_ref[...], g_ref[...], shift_ref[...], scale_ref[...]).astype(BF16)
    for c0 in range(0, o_ref.shape[1], _PROJ_COL_CHUNK):
        cols = slice(c0, c0 + _PROJ_COL_CHUNK)
        y = jnp.dot(h, w_ref[:, cols], preferred_element_type=F32)
        if c0 < q_cols:
            y = y * q_scale
        o_ref[:, cols] = y.astype(o_ref.dtype)


def _norm_proj(x, g, mod, w_bf, *, tm=_TM_RESIDENT):
    bsz, t_len, d = x.shape
    n_out = w_bf.shape[1]
    kern = functools.partial(_norm_proj_kernel, q_cols=d, q_scale=1.0 / math.sqrt(d // N_HEADS))
    return pl.pallas_call(
        kern,
        grid=(bsz, t_len // tm),
        in_specs=[
            pl.BlockSpec((None, tm, d), lambda b_, i: (b_, i, 0)),
            pl.BlockSpec((1, d), lambda b_, i: (0, 0)),
            pl.BlockSpec((None, 1, d), lambda b_, i: (b_, 0, 0)),
            pl.BlockSpec((None, 1, d), lambda b_, i: (b_, 0, 1)),
            pl.BlockSpec((d, n_out), lambda b_, i: (0, 0), pipeline_mode=pl.Buffered(1)),
        ],
        out_specs=pl.BlockSpec((None, tm, n_out), lambda b_, i: (b_, i, 0)),
        out_shape=jax.ShapeDtypeStruct((bsz, t_len, n_out), BF16),
        compiler_params=_params(2),
        name="norm_qkv_proj",
    )(x, g.reshape(1, d), mod, mod, w_bf)


_HEADS_PER_STEP = 4
_KEY_CHUNK = 256
_SUFFIX_BLOCK = 128
_SIGN_BIT = 0x80000000
_UNDERFLOW_SUM = 120.0


def _neg_abs(x):
    bits = lax.bitcast_convert_type(x, jnp.uint32) | jnp.uint32(_SIGN_BIT)
    return lax.bitcast_convert_type(bits, F32)


def _sb_attn_kernel(q_ref, k_ref, v_ref, o_ref, acc_ref, carry_ref, *, tq, dh, n_grp):
    i = pl.program_id(2)
    tk, sb = _KEY_CHUNK, _SUFFIX_BLOCK
    n_diag = tq // tk
    row = lax.broadcasted_iota(jnp.int32, (2 * sb, 2 * sb), 0)
    col = lax.broadcasted_iota(jnp.int32, (2 * sb, 2 * sb), 1)
    suffix_op = (((row % sb) >= col) | (col >= sb)).astype(BF16)
    qpos = i * tq + lax.broadcasted_iota(jnp.int32, (tq, tk), 0)
    kcol = lax.broadcasted_iota(jnp.int32, (tq, tk), 1)

    carry_ref[...] = jnp.zeros(carry_ref.shape, F32)
    acc_ref[...] = jnp.zeros(acc_ref.shape, F32)

    def chunk(kb, masked, row0=0):
        start = pl.multiple_of(kb * tk, tk)
        if masked:
            causal = ((start + kcol) < qpos)[row0:]
        for g in range(n_grp):
            lanes = slice(g * dh, (g + 1) * dh)
            q = q_ref[row0:, lanes]
            k = k_ref[pl.ds(start, tk), lanes]
            v = v_ref[pl.ds(start, tk), lanes]
            z = lax.dot_general(q, k, (((1,), (1,)), ((), ())), preferred_element_type=F32)
            p = jnp.maximum(z, 0.0) + jnp.log(1.0 + jnp.exp(_neg_abs(z)))
            if masked:
                p = jnp.where(causal, p, 0.0)
            hi = p.astype(BF16)
            lo = (p - hi.astype(F32)).astype(BF16)
            tail = carry_ref[g, row0:]
            logits = [None] * (tk // sb)
            for blk in reversed(range(tk // sb)):
                cols = slice(blk * sb, (blk + 1) * sb)
                stacked = jnp.concatenate([hi[:, cols], lo[:, cols]], axis=1)
                sums = jnp.dot(stacked, suffix_op, preferred_element_type=F32)
                logits[blk] = (z[:, cols] - tail) - sums[:, :sb]
                tail = tail + sums[:, sb:]
            a = jnp.exp(jnp.concatenate(logits, axis=1))
            if masked:
                a = jnp.where(causal, a, 0.0)
            acc_ref[g, row0:] += jnp.dot(a.astype(BF16), v, preferred_element_type=F32)
            carry_ref[g, row0:] = tail

    for dg in range(n_diag):
        chunk(i * n_diag + (n_diag - 1 - dg), True, row0=(n_diag - 1 - dg) * tk)

    n_off = i * n_diag

    def keep_going(state):
        n, live = state
        return jnp.logical_and(n < n_off, live)

    def body(state):
        n, _ = state
        chunk(n_off - 1 - n, False)
        return n + 1, jnp.min(carry_ref[...]) < _UNDERFLOW_SUM

    lax.while_loop(keep_going, body, (jnp.int32(0), jnp.bool_(True)))
    for g in range(n_grp):
        o_ref[:, g * dh:(g + 1) * dh] = acc_ref[g].astype(o_ref.dtype)


def _sb_attention(qkv, *, tq=_TQ_ATTENTION):
    bsz, t_len, d3 = qkv.shape
    d = d3 // 3
    dh = d // N_HEADS
    n_grp = _HEADS_PER_STEP
    wblk = n_grp * dh
    nblk = d // wblk
    kern = functools.partial(_sb_attn_kernel, tq=tq, dh=dh, n_grp=n_grp)
    return pl.pallas_call(
        kern,
        grid=(bsz, nblk, t_len // tq),
        in_specs=[
            pl.BlockSpec((None, tq, wblk), lambda b_, h, i: (b_, i, h)),
            pl.BlockSpec((None, t_len, wblk), lambda b_, h, i: (b_, 0, nblk + h)),
            pl.BlockSpec((None, t_len, wblk), lambda b_, h, i: (b_, 0, 2 * nblk + h)),
        ],
        out_specs=pl.BlockSpec((None, tq, wblk), lambda b_, h, i: (b_, i, h)),
        out_shape=jax.ShapeDtypeStruct((bsz, t_len, d), BF16),
        scratch_shapes=[
            pltpu.VMEM((n_grp, tq, dh), F32),
            pltpu.VMEM((n_grp, tq, _SUFFIX_BLOCK), F32),
        ],
        compiler_params=_params(3),
        name="stick_breaking_attention",
    )(qkv, qkv, qkv)


def kernel(x, c, mix_norm_g, mix_mod_w, mix_mod_b, cv_pw1_w, cv_pw1_b, cv_dw_w, cv_dw_b, cv_ln_g, cv_ln_b, cv_pw2_w, cv_pw2_b, sb_qkv_w, sb_o_w, ffn_norm_g, ffn_mod_w, ffn_mod_b, ffn_up_w, ffn_dw_w, ffn_dw_b, ffn_down_w, final_norm_g):
    bsz, t_len, d = x.shape
    depth = mix_norm_g.shape[0]
    c_pad = jnp.pad(c, ((0, (-bsz) % _SUBLANES), (0, 0)))
    mix_mod = _adaln(c_pad, mix_mod_w, mix_mod_b)[:, :bsz].reshape(depth, bsz, 1, 3 * d)
    ffn_mod = _adaln(c_pad, ffn_mod_w, ffn_mod_b)[:, :bsz].reshape(depth, bsz, 1, 3 * d)
    zero_bias = jnp.zeros((d,), F32)
    pw2_w, o_w, down_w = cv_pw2_w.astype(BF16), sb_o_w.astype(BF16), ffn_down_w.astype(BF16)

    for i in range(depth):
        j = i // 2
        if i % 2 == 0:
            conv = _conformer_in(x, mix_norm_g[i], mix_mod[i], cv_pw1_w[j].astype(BF16), cv_pw1_b[j],
                                 cv_dw_w[j], cv_dw_b[j])
            x = _proj_residual(conv, pw2_w, j, cv_pw2_b[j], x, mix_mod[i], ln=(cv_ln_g[j], cv_ln_b[j]))
        else:
            qkv = _norm_proj(x, mix_norm_g[i], mix_mod[i], sb_qkv_w[j].astype(BF16))
            o = _sb_attention(qkv)
            x = _proj_residual(o, o_w, j, zero_bias, x, mix_mod[i])
        act = _ffn_up(x, ffn_norm_g[i], ffn_mod[i], ffn_up_w, i, ffn_dw_w[i], ffn_dw_b[i])
        x = _proj_residual(act, down_w, i, zero_bias, x, ffn_mod[i],
                           final_g=final_norm_g if i == depth - 1 else None)
    return x
```
